```python
import jax, jax.numpy as jnp
from jax import lax
import numpy as np

D_MODEL = 2048
BATCH = 4
SEQ = 4096
DEPTH = 2

PLE_DIM = 256
D_FF = 4 * D_MODEL
EPS = 1e-6

SB_HEADS = 8
SB_HEAD_DIM = D_MODEL // (2 * SB_HEADS)
SB_BLOCK = 128
SB_W = SB_HEADS * SB_HEAD_DIM
GLA_HEADS = 8
GLA_DV = D_MODEL // (2 * GLA_HEADS)
GLA_DK = GLA_DV // 2
GLA_GATE_RANK = 16
GLA_TAU = 16.0
GLA_CHUNK = 64
GLA_K = GLA_HEADS * GLA_DK
GLA_V = GLA_HEADS * GLA_DV
AB_SPLITS = [SB_W, 2 * SB_W, 3 * SB_W, 3 * SB_W + GLA_K, 3 * SB_W + 2 * GLA_K,
             3 * SB_W + 2 * GLA_K + GLA_V, 3 * SB_W + 2 * GLA_K + GLA_V + GLA_GATE_RANK]
AB_IN = 3 * SB_W + 2 * GLA_K + 2 * GLA_V + GLA_GATE_RANK
AB_OUT = SB_W + GLA_V
SSD_INNER = 2 * D_MODEL
SSD_HEAD_DIM = 64
SSD_HEADS = SSD_INNER // SSD_HEAD_DIM
SSD_GROUPS = 8
SSD_STATE = 128
SSD_CONV = 4
SSD_CHUNK = 128
SSD_BC = SSD_GROUPS * SSD_STATE
SSD_CONV_DIM = SSD_INNER + 2 * SSD_BC
SSD_IN = SSD_INNER + SSD_CONV_DIM + SSD_HEADS

kernel_name = "hybrid_stickbreak_gla_ssd_block"

F32 = jnp.float32


def rms_norm(x, g):
    xf = x.astype(F32)
    y = xf * lax.rsqrt(jnp.mean(xf * xf, axis=-1, keepdims=True) + EPS)
    return (y * g.astype(F32)).astype(x.dtype)


def stick_breaking_attention(q, k, v):
    b, h, s, d = q.shape
    nb = s // SB_BLOCK
    scale = d ** -0.5
    kf = k.astype(F32)
    vf = v.astype(F32)
    qb = q.astype(F32).reshape(b, h, nb, SB_BLOCK, d).transpose(2, 0, 1, 3, 4)
    key_pos = jnp.arange(s)

    def block(args):
        qi, i = args
        z = jnp.einsum('bhtd,bhsd->bhts', qi, kf) * scale
        q_pos = i * SB_BLOCK + jnp.arange(SB_BLOCK)
        mask = key_pos[None, :] < q_pos[:, None]
        log_beta = jax.nn.log_sigmoid(z)
        log_1m = jnp.where(mask, log_beta - z, 0.0)
        after = lax.cumsum(log_1m, axis=3, reverse=True) - log_1m
        w = jnp.where(mask, jnp.exp(log_beta + after), 0.0)
        return jnp.einsum('bhts,bhsd->bhtd', w, vf)

    out = lax.map(block, (qb, jnp.arange(nb)))
    return out.transpose(1, 2, 0, 3, 4).reshape(b, h, s, d)


def gla_chunked(q, k, v, log_a):
    b, h, s, dk = q.shape
    dv = v.shape[-1]
    c = GLA_CHUNK
    nc = s // c
    f = lambda t: t.astype(F32).reshape(b, h, nc, c, t.shape[-1])
    q = f(q) * dk ** -0.5
    k = f(k)
    v = f(v)
    gcum = jnp.cumsum(f(log_a), axis=3)
    g_last = gcum[:, :, :, -1:, :]
    q_dec = q * jnp.exp(gcum)
    k_inv = k * jnp.exp(-gcum)
    k_end = k * jnp.exp(g_last - gcum)
    causal = jnp.tril(jnp.ones((c, c), dtype=bool))
    scores = jnp.where(causal, jnp.einsum('bhntd,bhnsd->bhnts', q_dec, k_inv), 0.0)
    o_intra = jnp.einsum('bhnts,bhnse->bhnte', scores, v)
    d_state = jnp.einsum('bhnsd,bhnse->nbhde', k_end, v)
    chunk_decay = jnp.exp(g_last[:, :, :, 0, :]).transpose(2, 0, 1, 3)

    def step(state, inp):
        dec, ds = inp
        return state * dec[..., None] + ds, state

    _, prev = lax.scan(step, jnp.zeros((b, h, dk, dv), F32), (chunk_decay, d_state))
    o_inter = jnp.einsum('bhntd,nbhde->bhnte', q_dec, prev)
    return (o_intra + o_inter).reshape(b, h, s, dv)


def sb_gla_mixer(x, w_in, w_gate_up, b_gate, gla_norm, w_out):
    b, s, _ = x.shape
    proj = x @ w_in
    sq, sk, sv, gq, gk, gv, glr, gout = jnp.split(proj, AB_SPLITS, axis=-1)
    heads = lambda t, n: t.reshape(b, s, n, -1).transpose(0, 2, 1, 3)
    o_sb = stick_breaking_attention(heads(sq, SB_HEADS), heads(sk, SB_HEADS), heads(sv, SB_HEADS))
    log_a = jax.nn.log_sigmoid((glr @ w_gate_up + b_gate).astype(F32)) / GLA_TAU
    o_gla = gla_chunked(heads(gq, GLA_HEADS), heads(gk, GLA_HEADS), heads(gv, GLA_HEADS),
                        heads(log_a, GLA_HEADS))
    o_gla = rms_norm(o_gla.transpose(0, 2, 1, 3), gla_norm) * \
        jax.nn.silu(gout.astype(F32)).reshape(b, s, GLA_HEADS, GLA_DV)
    o = jnp.concatenate([o_sb.transpose(0, 2, 1, 3).reshape(b, s, SB_W),
                         o_gla.reshape(b, s, GLA_V)], axis=-1).astype(x.dtype)
    return o @ w_out


def ssd_chunked(x, dt, a, bm, cm):
    b, s, nh, hp = x.shape
    g, n = bm.shape[2], bm.shape[3]
    r = nh // g
    c = SSD_CHUNK
    nc = s // c
    xdt = (x.astype(F32) * dt[..., None]).reshape(b, nc, c, g, r, hp)
    da = (dt * a).reshape(b, nc, c, g, r).transpose(0, 3, 4, 1, 2)
    bm = bm.astype(F32).reshape(b, nc, c, g, n)
    cm = cm.astype(F32).reshape(b, nc, c, g, n)
    a_cs = jnp.cumsum(da, axis=-1)
    causal = jnp.tril(jnp.ones((c, c), dtype=bool))
    seg = a_cs[..., :, None] - a_cs[..., None, :]
    L = jnp.exp(jnp.where(causal, seg, -jnp.inf))
    cb = jnp.einsum('bnlgk,bnsgk->bgnls', cm, bm)
    y_diag = jnp.einsum('bgnls,bgrnls,bnsgrp->bnlgrp', cb, L, xdt)
    decay_states = jnp.exp(a_cs[..., -1:] - a_cs)
    states = jnp.einsum('bnsgk,bgrns,bnsgrp->nbgrpk', bm, decay_states, xdt)
    chunk_decay = jnp.exp(a_cs[..., -1]).transpose(3, 0, 1, 2)

    def step(state, inp):
        dec, st = inp
        return state * dec[..., None, None] + st, state

    _, prev = lax.scan(step, jnp.zeros((b, g, r, hp, n), F32), (chunk_decay, states))
    y_off = jnp.einsum('bnlgk,nbgrpk,bgrnl->bnlgrp', cm, prev, jnp.exp(a_cs))
    return (y_diag + y_off).reshape(b, s, nh, hp)


def ssd_mixer(x, w_in, conv_w, conv_b, dt_bias, a_log, d_skip, norm_g, w_out):
    b, s, _ = x.shape
    proj = x @ w_in
    z, xbc, dt = jnp.split(proj, [SSD_INNER, SSD_INNER + SSD_CONV_DIM], axis=-1)
    xbc = lax.conv_general_dilated(xbc, conv_w[:, None, :], window_strides=(1,),
                                   padding=[(SSD_CONV - 1, 0)],
                                   dimension_numbers=('NWC', 'WIO', 'NWC'),
                                   feature_group_count=SSD_CONV_DIM) + conv_b
    xbc = jax.nn.silu(xbc)
    xs, bm, cm = jnp.split(xbc, [SSD_INNER, SSD_INNER + SSD_BC], axis=-1)
    dt = jax.nn.softplus(dt.astype(F32) + dt_bias.astype(F32))
    a = -jnp.exp(a_log.astype(F32))
    xs = xs.reshape(b, s, SSD_HEADS, SSD_HEAD_DIM)
    y = ssd_chunked(xs, dt, a, bm.reshape(b, s, SSD_GROUPS, SSD_STATE),
                    cm.reshape(b, s, SSD_GROUPS, SSD_STATE))
    y = y + d_skip.astype(F32)[:, None] * xs.astype(F32)
    y = y.reshape(b, s, SSD_INNER) * jax.nn.silu(z.astype(F32))
    y = rms_norm(y, norm_g).astype(x.dtype)
    return y @ w_out


def squared_relu_mlp(x, w_up, w_down):
    return jnp.square(jax.nn.relu(x @ w_up)) @ w_down


def setup_inputs(seed: int = 0) -> dict:
    key = jax.random.key(seed)
    ks = iter(jax.random.split(key, 32))
    n_even = (DEPTH + 1) // 2
    n_odd = DEPTH // 2

    def dense(shape, fan_in):
        return jax.random.normal(next(ks), shape, F32) * fan_in ** -0.5

    def gain(shape):
        return 1.0 + 0.02 * jax.random.normal(next(ks), shape, F32)

    def small(shape):
        return 0.01 * jax.random.normal(next(ks), shape, F32)

    x = jax.random.normal(next(ks), (BATCH, SEQ, D_MODEL), F32)
    p = jax.random.normal(next(ks), (DEPTH, BATCH, SEQ, PLE_DIM), F32)
    dt0 = jnp.exp(jax.random.uniform(next(ks), (n_odd, SSD_HEADS), F32,
                                     minval=float(np.log(1e-3)), maxval=float(np.log(1e-1))))
    ssd_dt_bias = dt0 + jnp.log(-jnp.expm1(-dt0))
    ssd_a_log = jnp.log(jax.random.uniform(next(ks), (n_odd, SSD_HEADS), F32, minval=1.0, maxval=16.0))
    return {
        "x": x,
        "p": p,
        "norm_mix": gain((DEPTH, D_MODEL)),
        "norm_mlp": gain((DEPTH, D_MODEL)),
        "ab_w_in": dense((n_even, D_MODEL, AB_IN), D_MODEL),
        "ab_w_gate_up": dense((n_even, GLA_GATE_RANK, GLA_K), GLA_GATE_RANK),
        "ab_b_gate": small((n_even, GLA_K)),
        "ab_gla_norm": gain((n_even, GLA_DV)),
        "ab_w_out": dense((n_even, AB_OUT, D_MODEL), AB_OUT),
        "ssd_w_in": dense((n_odd, D_MODEL, SSD_IN), D_MODEL),
        "ssd_conv_w": dense((n_odd, SSD_CONV, SSD_CONV_DIM), SSD_CONV),
        "ssd_conv_b": small((n_odd, SSD_CONV_DIM)),
        "ssd_dt_bias": ssd_dt_bias,
        "ssd_a_log": ssd_a_log,
        "ssd_d": gain((n_odd, SSD_HEADS)),
        "ssd_norm": gain((n_odd, SSD_INNER)),
        "ssd_w_out": dense((n_odd, SSD_INNER, D_MODEL), SSD_INNER),
        "mlp_w_up": dense((DEPTH, D_MODEL, D_FF), D_MODEL),
        "mlp_w_down": dense((DEPTH, D_FF, D_MODEL), D_FF),
        "ple_w_proj": dense((DEPTH, PLE_DIM, D_MODEL), PLE_DIM),
        "ple_w_gate": dense((DEPTH, D_MODEL, D_MODEL), D_MODEL),
        "final_norm": gain((D_MODEL,)),
    }


def reference(x, p, norm_mix, norm_mlp, ab_w_in, ab_w_gate_up, ab_b_gate, ab_gla_norm, ab_w_out,
              ssd_w_in, ssd_conv_w, ssd_conv_b, ssd_dt_bias, ssd_a_log, ssd_d, ssd_norm, ssd_w_out,
              mlp_w_up, mlp_w_down, ple_w_proj, ple_w_gate, final_norm):
    h = x
    for i in range(DEPTH):
        hn = rms_norm(h, norm_mix[i])
        j = i // 2
        if i % 2 == 0:
            h = h + sb_gla_mixer(hn, ab_w_in[j], ab_w_gate_up[j], ab_b_gate[j], ab_gla_norm[j],
                                 ab_w_out[j])
        else:
            h = h + ssd_mixer(hn, ssd_w_in[j], ssd_conv_w[j], ssd_conv_b[j], ssd_dt_bias[j],
                              ssd_a_log[j], ssd_d[j], ssd_norm[j], ssd_w_out[j])
        h = h + squared_relu_mlp(rms_norm(h, norm_mlp[i]), mlp_w_up[i], mlp_w_down[i])
        h = h + jax.nn.sigmoid(h @ ple_w_gate[i]) * (p[i] @ ple_w_proj[i])
    return rms_norm(h, final_norm)
```

```python
import functools

import numpy as np
import jax
import jax.numpy as jnp
from jax import lax
from jax.experimental import pallas as pl
from jax.experimental.pallas import tpu as pltpu

F32 = jnp.float32
BF16 = jnp.bfloat16

EPS = 1e-6
LANES = 128
VMEM_LIMIT = 56 * 1024 * 1024

SB_HEADS = 8
SB_DIM = 128
GLA_HEADS = 8
GLA_DK = 64
GLA_DV = 128
GLA_RANK = 16
GLA_TAU = 16.0
GLA_CHUNK = 64
SSD_HEAD_DIM = 64
SSD_GROUPS = 8
SSD_STATE = 128
SSD_CONV = 4
SSD_CHUNK = 128
CONV_HALO = 8


def _params(*sem):
    return pltpu.CompilerParams(dimension_semantics=sem, vmem_limit_bytes=VMEM_LIMIT)


def _dot(a, b):
    return jnp.dot(a, b, preferred_element_type=F32)


def _dot_nt(a, b):
    return lax.dot_general(a, b, (((1,), (1,)), ((), ())), preferred_element_type=F32)


def _dot_tn(a, b):
    return lax.dot_general(a, b, (((0,), (0,)), ((), ())), preferred_element_type=F32)


def _rms(x, g):
    ms = jnp.mean(x * x, axis=-1, keepdims=True)
    return x * lax.rsqrt(ms + EPS) * g


def _softplus_neg_abs(z):
    return jnp.log1p(jnp.exp(-jnp.abs(z)))


def _log_sigmoid(z):
    return jnp.minimum(z, 0.0) - _softplus_neg_abs(z)


def _silu(z):
    return z * (1.0 / (1.0 + jnp.exp(-z)))


def _split3(x):
    hi = x.astype(BF16)
    r = x - hi.astype(F32)
    mid = r.astype(BF16)
    lo = (r - mid.astype(F32)).astype(BF16)
    return hi, mid, lo


def _dot_exact_rhs(m01, x):
    hi, mid, lo = _split3(x)
    return _dot(m01, hi) + _dot(m01, mid) + _dot(m01, lo)


def _dot_exact_lhs(x, m01):
    hi, mid, lo = _split3(x)
    return _dot(hi, m01) + _dot(mid, m01) + _dot(lo, m01)


def _ab_inproj_kernel(x_ref, g_ref, w_ref, wglr_ref, wgu_ref, bg_ref, proj_ref, loga_ref, xn_ref):
    @pl.when(pl.program_id(1) == 0)
    def _():
        xb = _rms(x_ref[...], g_ref[...]).astype(BF16)
        xn_ref[...] = xb
        glr = _dot(xb, wglr_ref[...])
        ga = _dot(glr.astype(BF16), wgu_ref[...]) + bg_ref[...]
        loga_ref[...] = _log_sigmoid(ga) * (1.0 / GLA_TAU)

    proj_ref[...] = _dot(xn_ref[...], w_ref[...]).astype(BF16)


def _ab_inproj(x, g, w, wglr, wgu, bg, *, tm, tn):
    t, d = x.shape
    n = w.shape[1]
    gk = wgu.shape[1]
    return pl.pallas_call(
        _ab_inproj_kernel,
        out_shape=(jax.ShapeDtypeStruct((t, n), BF16), jax.ShapeDtypeStruct((t, gk), F32)),
        grid=(t // tm, n // tn),
        in_specs=[
            pl.BlockSpec((tm, d), lambda i, j: (i, 0)),
            pl.BlockSpec((1, d), lambda i, j: (0, 0)),
            pl.BlockSpec((d, tn), lambda i, j: (0, j)),
            pl.BlockSpec(wglr.shape, lambda i, j: (0, 0)),
            pl.BlockSpec(wgu.shape, lambda i, j: (0, 0)),
            pl.BlockSpec((1, gk), lambda i, j: (0, 0)),
        ],
        out_specs=(pl.BlockSpec((tm, tn), lambda i, j: (i, j)),
                   pl.BlockSpec((tm, gk), lambda i, j: (i, 0))),
        scratch_shapes=[pltpu.VMEM((tm, d), BF16)],
        compiler_params=_params("parallel", "arbitrary"),
        name="ab_inproj",
    )(x, g, w, wglr, wgu, bg)


def _ssd_inproj_kernel(x_ref, g_ref, w_ref, wdt_ref, proj_ref, dt_ref, xn_ref):
    @pl.when(pl.program_id(1) == 0)
    def _():
        xb = _rms(x_ref[...], g_ref[...]).astype(BF16)
        xn_ref[...] = xb
        dt_ref[...] = _dot(xb, wdt_ref[...])

    proj_ref[...] = _dot(xn_ref[...], w_ref[...]).astype(BF16)


def _ssd_inproj(x, g, w, wdt, *, tm, tn):
    t, d = x.shape
    n = w.shape[1]
    nd = wdt.shape[1]
    return pl.pallas_call(
        _ssd_inproj_kernel,
        out_shape=(jax.ShapeDtypeStruct((t, n), BF16), jax.ShapeDtypeStruct((t, nd), F32)),
        grid=(t // tm, n // tn),
        in_specs=[
            pl.BlockSpec((tm, d), lambda i, j: (i, 0)),
            pl.BlockSpec((1, d), lambda i, j: (0, 0)),
            pl.BlockSpec((d, tn), lambda i, j: (0, j)),
            pl.BlockSpec((d, nd), lambda i, j: (0, 0)),
        ],
        out_specs=(pl.BlockSpec((tm, tn), lambda i, j: (i, j)),
                   pl.BlockSpec((tm, nd), lambda i, j: (i, 0))),
        scratch_shapes=[pltpu.VMEM((tm, d), BF16)],
        compiler_params=_params("parallel", "arbitrary"),
        name="ssd_inproj",
    )(x, g, w, wdt)


def _sb_kernel(q_ref, k_ref, v_ref, o_ref, *, blk, scale):
    i = pl.program_id(2)
    q = q_ref[...]
    key = lax.broadcasted_iota(jnp.int32, (blk, blk), 0)
    qry = lax.broadcasted_iota(jnp.int32, (blk, blk), 1)
    later = jnp.where(qry > key, 1.0, 0.0).astype(BF16)
    causal = key < qry

    def tile(kb, carry, acc, masked):
        start = pl.multiple_of(kb * blk, blk)
        kblk = k_ref[pl.ds(start, blk), :]
        vblk = v_ref[pl.ds(start, blk), :]
        z = _dot_nt(kblk, q) * scale
        sp = _softplus_neg_abs(z)
        log_beta = jnp.minimum(z, 0.0) - sp
        l1 = jnp.minimum(-z, 0.0) - sp
        if masked:
            l1 = jnp.where(causal, l1, 0.0)
        hi = l1.astype(BF16)
        lo = (l1 - hi.astype(F32)).astype(BF16)
        after = _dot(later, hi) + _dot(later, lo) + carry
        w = jnp.exp(log_beta + after)
        if masked:
            w = jnp.where(causal, w, 0.0)
        acc = acc + _dot_tn(vblk, w.astype(BF16))
        carry = carry + jnp.sum(l1, axis=0, keepdims=True)
        return carry, acc

    carry = jnp.zeros((1, blk), F32)
    acc = jnp.zeros((q.shape[1], blk), F32)
    carry, acc = tile(i, carry, acc, True)

    def body(n, c):
        return tile(i - 1 - n, c[0], c[1], False)

    carry, acc = lax.fori_loop(0, i, body, (carry, acc))
    o_ref[...] = acc.T.astype(BF16)


def _sb_attention(proj, *, batch, seq, blk):
    t = proj.shape[0]
    nq = seq // blk
    kernel = functools.partial(_sb_kernel, blk=blk, scale=SB_DIM ** -0.5)
    return pl.pallas_call(
        kernel,
        out_shape=jax.ShapeDtypeStruct((t, SB_HEADS * SB_DIM), BF16),
        grid=(batch, SB_HEADS, nq),
        in_specs=[
            pl.BlockSpec((blk, SB_DIM), lambda b, h, i: (b * nq + i, h)),
            pl.BlockSpec((seq, SB_DIM), lambda b, h, i: (b, SB_HEADS + h)),
            pl.BlockSpec((seq, SB_DIM), lambda b, h, i: (b, 2 * SB_HEADS + h)),
        ],
        out_specs=pl.BlockSpec((blk, SB_DIM), lambda b, h, i: (b * nq + i, h)),
        compiler_params=_params("parallel", "parallel", "arbitrary"),
        name="sb_attention",
    )(proj, proj, proj)


def _gla_kernel(q_ref, k_ref, v_ref, go_ref, la_ref, gn_ref, o_ref, st_ref, *, nchunks):
    c = GLA_CHUNK

    @pl.when(pl.program_id(2) == 0)
    def _():
        st_ref[...] = jnp.zeros_like(st_ref)

    row = lax.broadcasted_iota(jnp.int32, (c, c), 0)
    col = lax.broadcasted_iota(jnp.int32, (c, c), 1)
    incl = jnp.where(col <= row, 1.0, 0.0).astype(BF16)
    causal = col <= row
    lane = lax.broadcasted_iota(jnp.int32, (1, 2 * GLA_DK), 1)
    head_lanes = (lane < GLA_DK, lane >= GLA_DK)
    srow = lax.broadcasted_iota(jnp.int32, (2 * GLA_DV, 2 * GLA_DK), 0)
    scol = lax.broadcasted_iota(jnp.int32, (2 * GLA_DV, 2 * GLA_DK), 1)
    same_head = (srow < GLA_DV) == (scol < GLA_DK)
    gn = gn_ref[...]

    for n in range(nchunks):
        sl = pl.ds(n * c, c)
        gcum = _dot_exact_rhs(incl, la_ref[sl, :])
        glast = gcum[c - 1:c, :]
        q = q_ref[sl, :].astype(F32) * (GLA_DK ** -0.5)
        k = k_ref[sl, :].astype(F32)
        v = v_ref[sl, :]
        q_dec = (q * jnp.exp(gcum)).astype(BF16)
        k_inv = (k * jnp.exp(-gcum)).astype(BF16)
        k_end = (k * jnp.exp(glast - gcum)).astype(BF16)
        st = st_ref[...]
        o_inter = _dot_nt(q_dec, st.astype(BF16))
        o_intra = []
        for a in range(2):
            qa = jnp.where(head_lanes[a], q_dec, jnp.zeros_like(q_dec))
            scores = jnp.where(causal, _dot_nt(qa, k_inv), 0.0)
            o_intra.append(_dot(scores.astype(BF16), v[:, a * GLA_DV:(a + 1) * GLA_DV]))
        st_ref[...] = st * jnp.exp(glast) + jnp.where(same_head, _dot_tn(v, k_end), 0.0)
        gate = _silu(go_ref[sl, :].astype(F32))
        outs = []
        for a in range(2):
            hs = slice(a * GLA_DV, (a + 1) * GLA_DV)
            o = o_intra[a] + o_inter[:, hs]
            outs.append(_rms(o, gn) * gate[:, hs])
        o_ref[sl, :] = jnp.concatenate(outs, axis=1).astype(BF16)


def _gla(proj, log_a, gnorm, *, batch, seq, ts):
    t = proj.shape[0]
    nt = seq // ts
    pairs = GLA_HEADS // 2
    kw, vw = 2 * GLA_DK, 2 * GLA_DV
    q0 = 3 * SB_HEADS * SB_DIM // kw
    k0 = q0 + GLA_HEADS * GLA_DK // kw
    v0 = (3 * SB_HEADS * SB_DIM + 2 * GLA_HEADS * GLA_DK) // vw
    g0 = v0 + GLA_HEADS * GLA_DV // vw
    kernel = functools.partial(_gla_kernel, nchunks=ts // GLA_CHUNK)
    return pl.pallas_call(
        kernel,
        out_shape=jax.ShapeDtypeStruct((t, GLA_HEADS * GLA_DV), BF16),
        grid=(batch, pairs, nt),
        in_specs=[
            pl.BlockSpec((ts, kw), lambda b, p, i: (b * nt + i, q0 + p)),
            pl.BlockSpec((ts, kw), lambda b, p, i: (b * nt + i, k0 + p)),
            pl.BlockSpec((ts, vw), lambda b, p, i: (b * nt + i, v0 + p)),
            pl.BlockSpec((ts, vw), lambda b, p, i: (b * nt + i, g0 + p)),
            pl.BlockSpec((ts, kw), lambda b, p, i: (b * nt + i, p)),
            pl.BlockSpec((1, GLA_DV), lambda b, p, i: (0, 0)),
        ],
        out_specs=pl.BlockSpec((ts, vw), lambda b, p, i: (b * nt + i, p)),
        scratch_shapes=[pltpu.VMEM((vw, kw), F32)],
        compiler_params=_params("parallel", "parallel", "arbitrary"),
        name="gla",
    )(proj, proj, proj, proj, log_a, gnorm)


def _ab_outproj_kernel(h_ref, a_ref, b_ref, w_ref, o_ref):
    ka = a_ref.shape[1]
    acc = _dot(a_ref[...], w_ref[0:ka, :]) + _dot(b_ref[...], w_ref[ka:, :])
    o_ref[...] = h_ref[...] + acc


def _ab_outproj(h, a, b, w, *, tm):
    t, d = h.shape
    return pl.pallas_call(
        _ab_outproj_kernel,
        out_shape=jax.ShapeDtypeStruct((t, d), F32),
        grid=(t // tm,),
        in_specs=[
            pl.BlockSpec((tm, d), lambda i: (i, 0)),
            pl.BlockSpec((tm, a.shape[1]), lambda i: (i, 0)),
            pl.BlockSpec((tm, b.shape[1]), lambda i: (i, 0)),
            pl.BlockSpec(w.shape, lambda i: (0, 0)),
        ],
        out_specs=pl.BlockSpec((tm, d), lambda i: (i, 0)),
        compiler_params=_params("parallel"),
        name="ab_outproj",
    )(h, a, b, w)


def _mlp_kernel(h_ref, g_ref, wu_ref, wd_ref, o_ref, xn_ref):
    @pl.when(pl.program_id(1) == 0)
    def _():
        h = h_ref[...]
        xn_ref[...] = _rms(h, g_ref[...]).astype(BF16)
        o_ref[...] = h

    u = jnp.maximum(_dot(xn_ref[...], wu_ref[...]), 0.0)
    o_ref[...] += _dot((u * u).astype(BF16), wd_ref[...])


def _mlp(h, g, wu, wd, *, tm, tf):
    t, d = h.shape
    f = wu.shape[1]
    return pl.pallas_call(
        _mlp_kernel,
        out_shape=jax.ShapeDtypeStruct((t, d), F32),
        grid=(t // tm, f // tf),
        in_specs=[
            pl.BlockSpec((tm, d), lambda i, j: (i, 0)),
            pl.BlockSpec((1, d), lambda i, j: (0, 0)),
            pl.BlockSpec((d, tf), lambda i, j: (0, j)),
            pl.BlockSpec((tf, d), lambda i, j: (j, 0)),
        ],
        out_specs=pl.BlockSpec((tm, d), lambda i, j: (i, 0)),
        scratch_shapes=[pltpu.VMEM((tm, d), BF16)],
        compiler_params=_params("parallel", "arbitrary"),
        name="mlp",
    )(h, g, wu, wd)


def _ple_kernel(h_ref, p_ref, wg_ref, wp_ref, fn_ref, o_ref, *, final_norm):
    h = h_ref[...]
    gate = _dot(h.astype(BF16), wg_ref[...])
    emb = _dot(p_ref[...].astype(BF16), wp_ref[...])
    out = h + emb * (1.0 / (1.0 + jnp.exp(-gate)))
    if final_norm:
        out = _rms(out, fn_ref[...])
    o_ref[...] = out


def _ple(h, p, wg, wp, fn, *, tm, final_norm):
    t, d = h.shape
    kernel = functools.partial(_ple_kernel, final_norm=final_norm)
    return pl.pallas_call(
        kernel,
        out_shape=jax.ShapeDtypeStruct((t, d), F32),
        grid=(t // tm,),
        in_specs=[
            pl.BlockSpec((tm, d), lambda i: (i, 0)),
            pl.BlockSpec((tm, p.shape[1]), lambda i: (i, 0)),
            pl.BlockSpec(wg.shape, lambda i: (0, 0)),
            pl.BlockSpec(wp.shape, lambda i: (0, 0)),
            pl.BlockSpec((1, d), lambda i: (0, 0)),
        ],
        out_specs=pl.BlockSpec((tm, d), lambda i: (i, 0)),
        compiler_params=_params("parallel"),
        name="ple",
    )(h, p, wg, wp, fn)


def _causal_conv_silu(raw_ref, pad_ref, w_ref, b_ref, first):
    ts = raw_ref.shape[0]

    @pl.when(first)
    def _():
        pad_ref[0:CONV_HALO, :] = jnp.zeros((CONV_HALO, pad_ref.shape[1]), F32)

    pad_ref[CONV_HALO:CONV_HALO + ts, :] = raw_ref[...].astype(F32)
    acc = b_ref[...] + w_ref[SSD_CONV - 1:SSD_CONV, :] * pad_ref[CONV_HALO:CONV_HALO + ts, :]
    for back in range(1, SSD_CONV):
        tap = SSD_CONV - 1 - back
        acc = acc + w_ref[tap:tap + 1, :] * pad_ref[CONV_HALO - back:CONV_HALO - back + ts, :]
    pad_ref[0:CONV_HALO, :] = pad_ref[ts:ts + CONV_HALO, :]
    return _silu(acc)


def _ssd_kernel(xs_ref, bm_ref, cm_ref, wx_ref, wb_ref, wc_ref, bx_ref, bb_ref, bc_ref,
                dt_ref, dtb_ref, alog_ref, sel_ref, exp_ref, dsk_ref, y_ref,
                st_ref, px_ref, pb_ref, pc_ref, *, nchunks, heads):
    c = SSD_CHUNK
    first = pl.program_id(2) == 0

    @pl.when(first)
    def _():
        st_ref[...] = jnp.zeros_like(st_ref)

    xs_all = _causal_conv_silu(xs_ref, px_ref, wx_ref, bx_ref, first)
    bm_all = _causal_conv_silu(bm_ref, pb_ref, wb_ref, bb_ref, first).astype(BF16)
    cm_all = _causal_conv_silu(cm_ref, pc_ref, wc_ref, bc_ref, first).astype(BF16)

    row = lax.broadcasted_iota(jnp.int32, (c, c), 0)
    col = lax.broadcasted_iota(jnp.int32, (c, c), 1)
    tril = col <= row
    incl = jnp.where(tril, 1.0, 0.0).astype(BF16)
    lane = lax.broadcasted_iota(jnp.int32, (1, LANES), 1)
    lo_half = lane < SSD_HEAD_DIM
    sel = sel_ref[0]
    expand = exp_ref[...]
    a = -jnp.exp(alog_ref[...])
    dsk = dsk_ref[...]

    for n in range(nchunks):
        sl = slice(n * c, (n + 1) * c)
        xs = xs_all[sl, :]
        bm = bm_all[sl, :]
        cm = cm_all[sl, :]
        z = dt_ref[sl, :] + dtb_ref[...]
        dt = jnp.maximum(z, 0.0) + _softplus_neg_abs(z)
        dtg = _dot_exact_lhs(dt, sel)
        dag = _dot_exact_lhs(dt * a, sel)
        a_cs = _dot_exact_rhs(incl, dag)
        a_cs_t = a_cs.T
        a_last = a_cs[c - 1:c, :]
        dt_x = _dot_exact_lhs(dtg, expand)
        ea_x = _dot_exact_lhs(jnp.exp(a_cs), expand)
        ds_x = _dot_exact_lhs(jnp.exp(a_last - a_cs), expand)
        xdt = xs * dt_x
        cb = _dot_nt(cm, bm)
        y_pairs = []
        for pr in range(heads // 2):
            xp = xdt[:, pr * LANES:(pr + 1) * LANES]
            ms = []
            for j in (2 * pr, 2 * pr + 1):
                seg = a_cs[:, j:j + 1] - a_cs_t[j:j + 1, :]
                decay = jnp.where(tril, jnp.exp(seg), 0.0)
                ms.append((cb * decay).astype(BF16))
            lhs = jnp.concatenate(ms, axis=1)
            rhs = jnp.concatenate([jnp.where(lo_half, xp, 0.0), jnp.where(lo_half, 0.0, xp)],
                                  axis=0).astype(BF16)
            y_pairs.append(_dot(lhs, rhs))
        y_diag = jnp.concatenate(y_pairs, axis=1)
        st = st_ref[...]
        y_off = _dot(cm, st.astype(BF16)) * ea_x
        st_ref[...] = st * ea_x[c - 1:c, :] + _dot_tn(bm, (xdt * ds_x).astype(BF16))
        y_ref[sl, :] = (y_diag + y_off + dsk * xs).astype(BF16)


def _ssd(proj, dt_raw, conv_w, conv_b, dt_bias, a_log, d_skip, *, batch, seq, ts, inner):
    t = proj.shape[0]
    nt = seq // ts
    g = SSD_GROUPS
    gw = inner // g
    heads = gw // SSD_HEAD_DIM
    n = SSD_STATE
    x0 = inner // gw
    b0 = 2 * inner // n
    c0 = b0 + g
    cb0 = inner // n
    cc0 = cb0 + g

    sel = np.zeros((g, LANES, LANES), np.float32)
    expand = np.zeros((LANES, gw), np.float32)
    for j in range(heads):
        for gi in range(g):
            sel[gi, gi * heads + j, j] = 1.0
        expand[j, j * SSD_HEAD_DIM:(j + 1) * SSD_HEAD_DIM] = 1.0
    sel = jnp.asarray(sel, BF16)
    expand = jnp.asarray(expand, BF16)

    kernel = functools.partial(_ssd_kernel, nchunks=ts // SSD_CHUNK, heads=heads)
    row = lambda b, gi, i: b * nt + i
    return pl.pallas_call(
        kernel,
        out_shape=jax.ShapeDtypeStruct((t, inner), BF16),
        grid=(batch, g, nt),
        in_specs=[
            pl.BlockSpec((ts, gw), lambda b, gi, i: (row(b, gi, i), x0 + gi)),
            pl.BlockSpec((ts, n), lambda b, gi, i: (row(b, gi, i), b0 + gi)),
            pl.BlockSpec((ts, n), lambda b, gi, i: (row(b, gi, i), c0 + gi)),
            pl.BlockSpec((SSD_CONV, gw), lambda b, gi, i: (0, gi)),
            pl.BlockSpec((SSD_CONV, n), lambda b, gi, i: (0, cb0 + gi)),
            pl.BlockSpec((SSD_CONV, n), lambda b, gi, i: (0, cc0 + gi)),
            pl.BlockSpec((1, gw), lambda b, gi, i: (0, gi)),
            pl.BlockSpec((1, n), lambda b, gi, i: (0, cb0 + gi)),
            pl.BlockSpec((1, n), lambda b, gi, i: (0, cc0 + gi)),
            pl.BlockSpec((ts, LANES), lambda b, gi, i: (row(b, gi, i), 0)),
            pl.BlockSpec((1, LANES), lambda b, gi, i: (0, 0)),
            pl.BlockSpec((1, LANES), lambda b, gi, i: (0, 0)),
            pl.BlockSpec((1, LANES, LANES), lambda b, gi, i: (gi, 0, 0)),
            pl.BlockSpec((LANES, gw), lambda b, gi, i: (0, 0)),
            pl.BlockSpec((1, gw), lambda b, gi, i: (0, gi)),
        ],
        out_specs=pl.BlockSpec((ts, gw), lambda b, gi, i: (row(b, gi, i), gi)),
        scratch_shapes=[
            pltpu.VMEM((n, gw), F32),
            pltpu.VMEM((ts + CONV_HALO, gw), F32),
            pltpu.VMEM((ts + CONV_HALO, n), F32),
            pltpu.VMEM((ts + CONV_HALO, n), F32),
        ],
        compiler_params=_params("parallel", "parallel", "arbitrary"),
        name="ssd",
    )(proj, proj, proj, conv_w, conv_w, conv_w, conv_b, conv_b, conv_b,
      dt_raw, dt_bias, a_log, sel, expand, d_skip)


def _ssd_outproj_kernel(h_ref, y_ref, z_ref, g_ref, w_ref, o_ref, yn_ref):
    @pl.when(pl.program_id(1) == 0)
    def _():
        y = y_ref[...].astype(F32) * _silu(z_ref[...].astype(F32))
        yn_ref[...] = _rms(y, g_ref[...]).astype(BF16)

    o_ref[...] = h_ref[...] + _dot(yn_ref[...], w_ref[...])


def _ssd_outproj(h, y, proj, g, w, *, tm, tn):
    t, d = h.shape
    inner = y.shape[1]
    return pl.pallas_call(
        _ssd_outproj_kernel,
        out_shape=jax.ShapeDtypeStruct((t, d), F32),
        grid=(t // tm, d // tn),
        in_specs=[
            pl.BlockSpec((tm, tn), lambda i, j: (i, j)),
            pl.BlockSpec((tm, inner), lambda i, j: (i, 0)),
            pl.BlockSpec((tm, inner), lambda i, j: (i, 0)),
            pl.BlockSpec((1, inner), lambda i, j: (0, 0)),
            pl.BlockSpec((inner, tn), lambda i, j: (0, j)),
        ],
        out_specs=pl.BlockSpec((tm, tn), lambda i, j: (i, j)),
        scratch_shapes=[pltpu.VMEM((tm, inner), BF16)],
        compiler_params=_params("parallel", "arbitrary"),
        name="ssd_outproj",
    )(h, y, proj, g, w)


def _pad_cols(w, n):
    return jnp.pad(w, ((0, 0), (0, n - w.shape[1])))


def kernel(x, p, norm_mix, norm_mlp, ab_w_in, ab_w_gate_up, ab_b_gate, ab_gla_norm, ab_w_out,
           ssd_w_in, ssd_conv_w, ssd_conv_b, ssd_dt_bias, ssd_a_log, ssd_d, ssd_norm, ssd_w_out,
           mlp_w_up, mlp_w_down, ple_w_proj, ple_w_gate, final_norm):
    batch, seq, d = x.shape
    t = batch * seq
    depth = p.shape[0]
    tm = 512
    h = x.reshape(t, d)
    row = lambda v: v.reshape(1, -1).astype(F32)

    for layer in range(depth):
        j = layer // 2
        if layer % 2 == 0:
            w = ab_w_in[j]
            glr0 = 3 * SB_HEADS * SB_DIM + 2 * GLA_HEADS * GLA_DK + GLA_HEADS * GLA_DV
            w_main = jnp.concatenate([w[:, :glr0], w[:, glr0 + GLA_RANK:]], axis=1).astype(BF16)
            w_glr = _pad_cols(w[:, glr0:glr0 + GLA_RANK], LANES).astype(BF16)
            w_gu = jnp.pad(ab_w_gate_up[j], ((0, LANES - GLA_RANK), (0, 0))).astype(BF16)
            proj, log_a = _ab_inproj(h, row(norm_mix[layer]), w_main, w_glr, w_gu,
                                     row(ab_b_gate[j]), tm=tm, tn=1024)
            o_sb = _sb_attention(proj, batch=batch, seq=seq, blk=256)
            o_gla = _gla(proj, log_a, row(ab_gla_norm[j]), batch=batch, seq=seq, ts=256)
            h = _ab_outproj(h, o_sb, o_gla, ab_w_out[j].astype(BF16), tm=tm)
        else:
            w = ssd_w_in[j]
            inner = ssd_w_out.shape[1]
            n_zx = w.shape[1] - inner // SSD_HEAD_DIM
            proj, dt_raw = _ssd_inproj(h, row(norm_mix[layer]), w[:, :n_zx].astype(BF16),
                                       _pad_cols(w[:, n_zx:], LANES).astype(BF16), tm=tm, tn=1024)
            y = _ssd(proj, dt_raw, ssd_conv_w[j], row(ssd_conv_b[j]),
                     _pad_cols(row(ssd_dt_bias[j]), LANES), _pad_cols(row(ssd_a_log[j]), LANES),
                     row(jnp.repeat(ssd_d[j], SSD_HEAD_DIM)), batch=batch, seq=seq, ts=256,
                     inner=inner)
            h = _ssd_outproj(h, y, proj, row(ssd_norm[j]), ssd_w_out[j].astype(BF16),
                             tm=256, tn=1024)
        h = _mlp(h, row(norm_mlp[layer]), mlp_w_up[layer].astype(BF16),
                 mlp_w_down[layer].astype(BF16), tm=tm, tf=512)
        h = _ple(h, p[layer].reshape(t, -1), ple_w_gate[layer].astype(BF16),
                 ple_w_proj[layer].astype(BF16), row(final_norm), tm=256,
                 final_norm=layer == depth - 1)
    return h.reshape(batch, seq, d)
```

```python
import functools

import numpy as np
import jax
import jax.numpy as jnp
from jax import lax
from jax.experimental import pallas as pl
from jax.experimental.pallas import tpu as pltpu

F32 = jnp.float32
BF16 = jnp.bfloat16

EPS = 1e-6
LOG2E = 1.4426950408889634
LANES = 128
VMEM_LIMIT = 56 * 1024 * 1024

SB_HEADS = 8
SB_DIM = 128
GLA_HEADS = 8
GLA_DK = 64
GLA_DV = 128
GLA_RANK = 16
GLA_TAU = 16.0
GLA_CHUNK = 64
SSD_HEAD_DIM = 64
SSD_GROUPS = 8
SSD_STATE = 128
SSD_CONV = 4
SSD_CHUNK = 128
CONV_HALO = 8


def _params(*sem):
    return pltpu.CompilerParams(dimension_semantics=sem, vmem_limit_bytes=VMEM_LIMIT)


def _dot(a, b):
    return jnp.dot(a, b, preferred_element_type=F32)


def _dot_nt(a, b):
    return lax.dot_general(a, b, (((1,), (1,)), ((), ())), preferred_element_type=F32)


def _dot_tn(a, b):
    return lax.dot_general(a, b, (((0,), (0,)), ((), ())), preferred_element_type=F32)


def _rms(x, g):
    ms = jnp.mean(x * x, axis=-1, keepdims=True)
    return x * lax.rsqrt(ms + EPS) * g


def _softplus_neg_abs(z):
    return jnp.log(1.0 + jnp.exp(-jnp.abs(z)))


def _log_sigmoid(z):
    return jnp.minimum(z, 0.0) - _softplus_neg_abs(z)


def _silu(z):
    return z * (1.0 / (1.0 + jnp.exp(-z)))


def _split3(x):
    hi = x.astype(BF16)
    r = x - hi.astype(F32)
    mid = r.astype(BF16)
    lo = (r - mid.astype(F32)).astype(BF16)
    return hi, mid, lo


def _dot_exact_rhs(m01, x):
    hi, mid, lo = _split3(x)
    return _dot(m01, hi) + _dot(m01, mid) + _dot(m01, lo)


def _dot_exact_lhs(x, m01):
    hi, mid, lo = _split3(x)
    return _dot(hi, m01) + _dot(mid, m01) + _dot(lo, m01)


def _ab_inproj_kernel(x_ref, g_ref, w_ref, wglr_ref, wgu_ref, bg_ref, proj_ref, loga_ref, xn_ref):
    @pl.when(pl.program_id(1) == 0)
    def _():
        xb = _rms(x_ref[...], g_ref[...]).astype(BF16)
        xn_ref[...] = xb
        glr = _dot(xb, wglr_ref[...])
        ga = _dot(glr.astype(BF16), wgu_ref[...]) + bg_ref[...]
        loga_ref[...] = _log_sigmoid(ga) * (1.0 / GLA_TAU)

    proj_ref[...] = _dot(xn_ref[...], w_ref[...]).astype(BF16)


def _ab_inproj(x, g, w, wglr, wgu, bg, *, tm, tn):
    t, d = x.shape
    n = w.shape[1]
    gk = wgu.shape[1]
    return pl.pallas_call(
        _ab_inproj_kernel,
        out_shape=(jax.ShapeDtypeStruct((t, n), BF16), jax.ShapeDtypeStruct((t, gk), F32)),
        grid=(t // tm, n // tn),
        in_specs=[
            pl.BlockSpec((tm, d), lambda i, j: (i, 0)),
            pl.BlockSpec((1, d), lambda i, j: (0, 0)),
            pl.BlockSpec((d, tn), lambda i, j: (0, j)),
            pl.BlockSpec(wglr.shape, lambda i, j: (0, 0)),
            pl.BlockSpec(wgu.shape, lambda i, j: (0, 0)),
            pl.BlockSpec((1, gk), lambda i, j: (0, 0)),
        ],
        out_specs=(pl.BlockSpec((tm, tn), lambda i, j: (i, j)),
                   pl.BlockSpec((tm, gk), lambda i, j: (i, 0))),
        scratch_shapes=[pltpu.VMEM((tm, d), BF16)],
        compiler_params=_params("parallel", "arbitrary"),
        name="ab_inproj",
    )(x, g, w, wglr, wgu, bg)


def _ssd_inproj_kernel(x_ref, g_ref, w_ref, wdt_ref, proj_ref, dt_ref, xn_ref):
    @pl.when(pl.program_id(1) == 0)
    def _():
        xb = _rms(x_ref[...], g_ref[...]).astype(BF16)
        xn_ref[...] = xb
        dt_ref[...] = _dot(xb, wdt_ref[...])

    proj_ref[...] = _dot(xn_ref[...], w_ref[...]).astype(BF16)


def _ssd_inproj(x, g, w, wdt, *, tm, tn):
    t, d = x.shape
    n = w.shape[1]
    nd = wdt.shape[1]
    return pl.pallas_call(
        _ssd_inproj_kernel,
        out_shape=(jax.ShapeDtypeStruct((t, n), BF16), jax.ShapeDtypeStruct((t, nd), F32)),
        grid=(t // tm, n // tn),
        in_specs=[
            pl.BlockSpec((tm, d), lambda i, j: (i, 0)),
            pl.BlockSpec((1, d), lambda i, j: (0, 0)),
            pl.BlockSpec((d, tn), lambda i, j: (0, j)),
            pl.BlockSpec((d, nd), lambda i, j: (0, 0)),
        ],
        out_specs=(pl.BlockSpec((tm, tn), lambda i, j: (i, j)),
                   pl.BlockSpec((tm, nd), lambda i, j: (i, 0))),
        scratch_shapes=[pltpu.VMEM((tm, d), BF16)],
        compiler_params=_params("parallel", "arbitrary"),
        name="ssd_inproj",
    )(x, g, w, wdt)


def _sb_kernel(q_ref, k_ref, v_ref, o_ref, *, bq, bk, scale):
    i = pl.program_id(2)
    nd = bq // bk
    q = q_ref[...]
    key = lax.broadcasted_iota(jnp.int32, (bk, bq), 0)
    qry = lax.broadcasted_iota(jnp.int32, (bk, bq), 1)
    col = lax.broadcasted_iota(jnp.int32, (bk, 2 * bk), 1)
    row = lax.broadcasted_iota(jnp.int32, (bk, 2 * bk), 0)
    later2 = jnp.where(jnp.where(col >= bk, col - bk, col) > row, 1.0, 0.0).astype(BF16)

    def tile(kb, carry, acc, key_offset):
        start = pl.multiple_of(kb * bk, bk)
        kblk = k_ref[pl.ds(start, bk), :]
        vblk = v_ref[pl.ds(start, bk), :]
        z2 = _dot_nt(kblk, q) * (scale * LOG2E)
        neg = jnp.minimum(-z2, 0.0)
        pos = jnp.minimum(z2, 0.0)
        sp = jnp.log2(1.0 + jnp.exp2(neg + pos))
        log_beta = pos - sp
        l1 = neg - sp
        if key_offset is not None:
            causal = key + key_offset < qry
            l1 = jnp.where(causal, l1, 0.0)
        hi = l1.astype(BF16)
        lo = (l1 - hi.astype(F32)).astype(BF16)
        within = _dot(later2, jnp.concatenate([hi, lo], axis=0))
        w = jnp.exp2(log_beta + (within + carry))
        if key_offset is not None:
            w = jnp.where(causal, w, 0.0)
        acc = acc + _dot_tn(vblk, w.astype(BF16))
        carry = carry + (within[0:1, :] + l1[0:1, :])
        return carry, acc

    carry = jnp.zeros((1, bq), F32)
    acc = jnp.zeros((q.shape[1], bq), F32)
    for r in reversed(range(nd)):
        carry, acc = tile(nd * i + r, carry, acc, r * bk)

    def body(n, c):
        for r in reversed(range(nd)):
            c = tile(nd * (i - 1 - n) + r, c[0], c[1], None)
        return c

    carry, acc = lax.fori_loop(0, i, body, (carry, acc))
    o_ref[...] = acc.T.astype(BF16)


def _sb_attention(proj, *, batch, seq, bq, bk):
    t = proj.shape[0]
    nq = seq // bq
    kernel = functools.partial(_sb_kernel, bq=bq, bk=bk, scale=SB_DIM ** -0.5)
    return pl.pallas_call(
        kernel,
        out_shape=jax.ShapeDtypeStruct((t, SB_HEADS * SB_DIM), BF16),
        grid=(batch, SB_HEADS, nq),
        in_specs=[
            pl.BlockSpec((bq, SB_DIM), lambda b, h, i: (b * nq + i, h)),
            pl.BlockSpec((seq, SB_DIM), lambda b, h, i: (b, SB_HEADS + h)),
            pl.BlockSpec((seq, SB_DIM), lambda b, h, i: (b, 2 * SB_HEADS + h)),
        ],
        out_specs=pl.BlockSpec((bq, SB_DIM), lambda b, h, i: (b * nq + i, h)),
        compiler_params=_params("parallel", "parallel", "arbitrary"),
        name="sb_attention",
    )(proj, proj, proj)


def _gla_kernel(q_ref, k_ref, v_ref, go_ref, la_ref, gn_ref, o_ref, st_ref, *, nchunks):
    c = GLA_CHUNK

    @pl.when(pl.program_id(2) == 0)
    def _():
        st_ref[...] = jnp.zeros_like(st_ref)

    row = lax.broadcasted_iota(jnp.int32, (c, c), 0)
    col = lax.broadcasted_iota(jnp.int32, (c, c), 1)
    incl = jnp.where(col <= row, 1.0, 0.0).astype(BF16)
    causal = col <= row
    lane = lax.broadcasted_iota(jnp.int32, (1, 2 * GLA_DK), 1)
    head_lanes = (lane < GLA_DK, lane >= GLA_DK)
    srow = lax.broadcasted_iota(jnp.int32, (2 * GLA_DV, 2 * GLA_DK), 0)
    scol = lax.broadcasted_iota(jnp.int32, (2 * GLA_DV, 2 * GLA_DK), 1)
    same_head = (srow < GLA_DV) == (scol < GLA_DK)
    gn = gn_ref[...]

    for n in range(nchunks):
        sl = pl.ds(n * c, c)
        gcum = _dot_exact_rhs(incl, la_ref[sl, :])
        glast = gcum[c - 1:c, :]
        q = q_ref[sl, :].astype(F32) * (GLA_DK ** -0.5)
        k = k_ref[sl, :].astype(F32)
        v = v_ref[sl, :]
        q_dec = (q * jnp.exp(gcum)).astype(BF16)
        k_inv = (k * jnp.exp(-gcum)).astype(BF16)
        k_end = (k * jnp.exp(glast - gcum)).astype(BF16)
        st = st_ref[...]
        o_inter = _dot_nt(q_dec, st.astype(BF16))
        o_intra = []
        for a in range(2):
            qa = jnp.where(head_lanes[a], q_dec, jnp.zeros_like(q_dec))
            scores = jnp.where(causal, _dot_nt(qa, k_inv), 0.0)
            o_intra.append(_dot(scores.astype(BF16), v[:, a * GLA_DV:(a + 1) * GLA_DV]))
        st_ref[...] = st * jnp.exp(glast) + jnp.where(same_head, _dot_tn(v, k_end), 0.0)
        gate = _silu(go_ref[sl, :].astype(F32))
        outs = []
        for a in range(2):
            hs = slice(a * GLA_DV, (a + 1) * GLA_DV)
            o = o_intra[a] + o_inter[:, hs]
            outs.append(_rms(o, gn) * gate[:, hs])
        o_ref[sl, :] = jnp.concatenate(outs, axis=1).astype(BF16)


def _gla(proj, log_a, gnorm, *, batch, seq, ts):
    t = proj.shape[0]
    nt = seq // ts
    pairs = GLA_HEADS // 2
    kw, vw = 2 * GLA_DK, 2 * GLA_DV
    q0 = 3 * SB_HEADS * SB_DIM // kw
    k0 = q0 + GLA_HEADS * GLA_DK // kw
    v0 = (3 * SB_HEADS * SB_DIM + 2 * GLA_HEADS * GLA_DK) // vw
    g0 = v0 + GLA_HEADS * GLA_DV // vw
    kernel = functools.partial(_gla_kernel, nchunks=ts // GLA_CHUNK)
    return pl.pallas_call(
        kernel,
        out_shape=jax.ShapeDtypeStruct((t, GLA_HEADS * GLA_DV), BF16),
        grid=(batch, pairs, nt),
        in_specs=[
            pl.BlockSpec((ts, kw), lambda b, p, i: (b * nt + i, q0 + p)),
            pl.BlockSpec((ts, kw), lambda b, p, i: (b * nt + i, k0 + p)),
            pl.BlockSpec((ts, vw), lambda b, p, i: (b * nt + i, v0 + p)),
            pl.BlockSpec((ts, vw), lambda b, p, i: (b * nt + i, g0 + p)),
            pl.BlockSpec((ts, kw), lambda b, p, i: (b * nt + i, p)),
            pl.BlockSpec((1, GLA_DV), lambda b, p, i: (0, 0)),
        ],
        out_specs=pl.BlockSpec((ts, vw), lambda b, p, i: (b * nt + i, p)),
        scratch_shapes=[pltpu.VMEM((vw, kw), F32)],
        compiler_params=_params("parallel", "parallel", "arbitrary"),
        name="gla",
    )(proj, proj, proj, proj, log_a, gnorm)


def _ab_outproj_kernel(h_ref, a_ref, b_ref, w_ref, o_ref):
    ka = a_ref.shape[1]
    acc = _dot(a_ref[...], w_ref[0:ka, :]) + _dot(b_ref[...], w_ref[ka:, :])
    o_ref[...] = h_ref[...] + acc


def _ab_outproj(h, a, b, w, *, tm):
    t, d = h.shape
    return pl.pallas_call(
        _ab_outproj_kernel,
        out_shape=jax.ShapeDtypeStruct((t, d), F32),
        grid=(t // tm,),
        in_specs=[
            pl.BlockSpec((tm, d), lambda i: (i, 0)),
            pl.BlockSpec((tm, a.shape[1]), lambda i: (i, 0)),
            pl.BlockSpec((tm, b.shape[1]), lambda i: (i, 0)),
            pl.BlockSpec(w.shape, lambda i: (0, 0)),
        ],
        out_specs=pl.BlockSpec((tm, d), lambda i: (i, 0)),
        compiler_params=_params("parallel"),
        name="ab_outproj",
    )(h, a, b, w)


def _mlp_kernel(h_ref, g_ref, wu_ref, wd_ref, o_ref, xn_ref):
    @pl.when(pl.program_id(1) == 0)
    def _():
        h = h_ref[...]
        xn_ref[...] = _rms(h, g_ref[...]).astype(BF16)
        o_ref[...] = h

    u = jnp.maximum(_dot(xn_ref[...], wu_ref[...]), 0.0)
    o_ref[...] += _dot((u * u).astype(BF16), wd_ref[...])


def _mlp(h, g, wu, wd, *, tm, tf):
    t, d = h.shape
    f = wu.shape[1]
    return pl.pallas_call(
        _mlp_kernel,
        out_shape=jax.ShapeDtypeStruct((t, d), F32),
        grid=(t // tm, f // tf),
        in_specs=[
            pl.BlockSpec((tm, d), lambda i, j: (i, 0)),
            pl.BlockSpec((1, d), lambda i, j: (0, 0)),
            pl.BlockSpec((d, tf), lambda i, j: (0, j)),
            pl.BlockSpec((tf, d), lambda i, j: (j, 0)),
        ],
        out_specs=pl.BlockSpec((tm, d), lambda i, j: (i, 0)),
        scratch_shapes=[pltpu.VMEM((tm, d), BF16)],
        compiler_params=_params("parallel", "arbitrary"),
        name="mlp",
    )(h, g, wu, wd)


def _ple_kernel(h_ref, p_ref, wg_ref, wp_ref, fn_ref, o_ref, *, final_norm):
    h = h_ref[...]
    gate = _dot(h.astype(BF16), wg_ref[...])
    emb = _dot(p_ref[...].astype(BF16), wp_ref[...])
    out = h + emb * (1.0 / (1.0 + jnp.exp(-gate)))
    if final_norm:
        out = _rms(out, fn_ref[...])
    o_ref[...] = out


def _ple(h, p, wg, wp, fn, *, tm, final_norm):
    t, d = h.shape
    kernel = functools.partial(_ple_kernel, final_norm=final_norm)
    return pl.pallas_call(
        kernel,
        out_shape=jax.ShapeDtypeStruct((t, d), F32),
        grid=(t // tm,),
        in_specs=[
            pl.BlockSpec((tm, d), lambda i: (i, 0)),
            pl.BlockSpec((tm, p.shape[1]), lambda i: (i, 0)),
            pl.BlockSpec(wg.shape, lambda i: (0, 0)),
            pl.BlockSpec(wp.shape, lambda i: (0, 0)),
            pl.BlockSpec((1, d), lambda i: (0, 0)),
        ],
        out_specs=pl.BlockSpec((tm, d), lambda i: (i, 0)),
        compiler_params=_params("parallel"),
        name="ple",
    )(h, p, wg, wp, fn)


def _causal_conv_silu(raw_ref, pad_ref, w_ref, b_ref, first):
    ts = raw_ref.shape[0]

    @pl.when(first)
    def _():
        pad_ref[0:CONV_HALO, :] = jnp.zeros((CONV_HALO, pad_ref.shape[1]), F32)

    pad_ref[CONV_HALO:CONV_HALO + ts, :] = raw_ref[...].astype(F32)
    acc = b_ref[...] + w_ref[SSD_CONV - 1:SSD_CONV, :] * pad_ref[CONV_HALO:CONV_HALO + ts, :]
    for back in range(1, SSD_CONV):
        tap = SSD_CONV - 1 - back
        acc = acc + w_ref[tap:tap + 1, :] * pad_ref[CONV_HALO - back:CONV_HALO - back + ts, :]
    pad_ref[0:CONV_HALO, :] = pad_ref[ts:ts + CONV_HALO, :]
    return _silu(acc)


def _ssd_kernel(xs_ref, bm_ref, cm_ref, wx_ref, wb_ref, wc_ref, bx_ref, bb_ref, bc_ref,
                dt_ref, dtb_ref, alog_ref, sel_ref, exp_ref, dsk_ref, y_ref,
                st_ref, px_ref, pb_ref, pc_ref, *, nchunks, heads):
    c = SSD_CHUNK
    first = pl.program_id(2) == 0

    @pl.when(first)
    def _():
        st_ref[...] = jnp.zeros_like(st_ref)

    xs_all = _causal_conv_silu(xs_ref, px_ref, wx_ref, bx_ref, first)
    bm_all = _causal_conv_silu(bm_ref, pb_ref, wb_ref, bb_ref, first).astype(BF16)
    cm_all = _causal_conv_silu(cm_ref, pc_ref, wc_ref, bc_ref, first).astype(BF16)

    row = lax.broadcasted_iota(jnp.int32, (c, c), 0)
    col = lax.broadcasted_iota(jnp.int32, (c, c), 1)
    tril = col <= row
    incl = jnp.where(tril, 1.0, 0.0).astype(BF16)
    lane = lax.broadcasted_iota(jnp.int32, (1, LANES), 1)
    lo_half = lane < SSD_HEAD_DIM
    sel = sel_ref[0]
    expand = exp_ref[...]
    a = -jnp.exp(alog_ref[...])
    dsk = dsk_ref[...]

    for n in range(nchunks):
        sl = slice(n * c, (n + 1) * c)
        xs = xs_all[sl, :]
        bm = bm_all[sl, :]
        cm = cm_all[sl, :]
        z = dt_ref[sl, :] + dtb_ref[...]
        dt = jnp.maximum(z, 0.0) + _softplus_neg_abs(z)
        dtg = _dot_exact_lhs(dt, sel)
        dag = _dot_exact_lhs(dt * a, sel)
        a_cs = _dot_exact_rhs(incl, dag)
        a_cs_t = a_cs.T
        a_last = a_cs[c - 1:c, :]
        dt_x = _dot_exact_lhs(dtg, expand)
        ea_x = _dot_exact_lhs(jnp.exp(a_cs), expand)
        ds_x = _dot_exact_lhs(jnp.exp(a_last - a_cs), expand)
        xdt = xs * dt_x
        cb = _dot_nt(cm, bm)
        y_pairs = []
        for pr in range(heads // 2):
            xp = xdt[:, pr * LANES:(pr + 1) * LANES]
            ms = []
            for j in (2 * pr, 2 * pr + 1):
                seg = a_cs[:, j:j + 1] - a_cs_t[j:j + 1, :]
                decay = jnp.where(tril, jnp.exp(seg), 0.0)
                ms.append((cb * decay).astype(BF16))
            lhs = jnp.concatenate(ms, axis=1)
            rhs = jnp.concatenate([jnp.where(lo_half, xp, 0.0), jnp.where(lo_half, 0.0, xp)],
                                  axis=0).astype(BF16)
            y_pairs.append(_dot(lhs, rhs))
        y_diag = jnp.concatenate(y_pairs, axis=1)
        st = st_ref[...]
        y_off = _dot(cm, st.astype(BF16)) * ea_x
        st_ref[...] = st * ea_x[c - 1:c, :] + _dot_tn(bm, (xdt * ds_x).astype(BF16))
        y_ref[sl, :] = (y_diag + y_off + dsk * xs).astype(BF16)


def _ssd(proj, dt_raw, conv_w, conv_b, dt_bias, a_log, d_skip, *, batch, seq, ts, inner):
    t = proj.shape[0]
    nt = seq // ts
    g = SSD_GROUPS
    gw = inner // g
    heads = gw // SSD_HEAD_DIM
    n = SSD_STATE
    x0 = inner // gw
    b0 = 2 * inner // n
    c0 = b0 + g
    cb0 = inner // n
    cc0 = cb0 + g

    sel = np.zeros((g, LANES, LANES), np.float32)
    expand = np.zeros((LANES, gw), np.float32)
    for j in range(heads):
        for gi in range(g):
            sel[gi, gi * heads + j, j] = 1.0
        expand[j, j * SSD_HEAD_DIM:(j + 1) * SSD_HEAD_DIM] = 1.0
    sel = jnp.asarray(sel, BF16)
    expand = jnp.asarray(expand, BF16)

    kernel = functools.partial(_ssd_kernel, nchunks=ts // SSD_CHUNK, heads=heads)
    row = lambda b, gi, i: b * nt + i
    return pl.pallas_call(
        kernel,
        out_shape=jax.ShapeDtypeStruct((t, inner), BF16),
        grid=(batch, g, nt),
        in_specs=[
            pl.BlockSpec((ts, gw), lambda b, gi, i: (row(b, gi, i), x0 + gi)),
            pl.BlockSpec((ts, n), lambda b, gi, i: (row(b, gi, i), b0 + gi)),
            pl.BlockSpec((ts, n), lambda b, gi, i: (row(b, gi, i), c0 + gi)),
            pl.BlockSpec((SSD_CONV, gw), lambda b, gi, i: (0, gi)),
            pl.BlockSpec((SSD_CONV, n), lambda b, gi, i: (0, cb0 + gi)),
            pl.BlockSpec((SSD_CONV, n), lambda b, gi, i: (0, cc0 + gi)),
            pl.BlockSpec((1, gw), lambda b, gi, i: (0, gi)),
            pl.BlockSpec((1, n), lambda b, gi, i: (0, cb0 + gi)),
            pl.BlockSpec((1, n), lambda b, gi, i: (0, cc0 + gi)),
            pl.BlockSpec((ts, LANES), lambda b, gi, i: (row(b, gi, i), 0)),
            pl.BlockSpec((1, LANES), lambda b, gi, i: (0, 0)),
            pl.BlockSpec((1, LANES), lambda b, gi, i: (0, 0)),
            pl.BlockSpec((1, LANES, LANES), lambda b, gi, i: (gi, 0, 0)),
            pl.BlockSpec((LANES, gw), lambda b, gi, i: (0, 0)),
            pl.BlockSpec((1, gw), lambda b, gi, i: (0, gi)),
        ],
        out_specs=pl.BlockSpec((ts, gw), lambda b, gi, i: (row(b, gi, i), gi)),
        scratch_shapes=[
            pltpu.VMEM((n, gw), F32),
            pltpu.VMEM((ts + CONV_HALO, gw), F32),
            pltpu.VMEM((ts + CONV_HALO, n), F32),
            pltpu.VMEM((ts + CONV_HALO, n), F32),
        ],
        compiler_params=_params("parallel", "parallel", "arbitrary"),
        name="ssd",
    )(proj, proj, proj, conv_w, conv_w, conv_w, conv_b, conv_b, conv_b,
      dt_raw, dt_bias, a_log, sel, expand, d_skip)


def _ssd_outproj_kernel(h_ref, y_ref, z_ref, g_ref, w_ref, o_ref, yn_ref):
    @pl.when(pl.program_id(1) == 0)
    def _():
        y = y_ref[...].astype(F32) * _silu(z_ref[...].astype(F32))
        yn_ref[...] = _rms(y, g_ref[...]).astype(BF16)

    o_ref[...] = h_ref[...] + _dot(yn_ref[...], w_ref[...])


def _ssd_outproj(h, y, proj, g, w, *, tm, tn):
    t, d = h.shape
    inner = y.shape[1]
    return pl.pallas_call(
        _ssd_outproj_kernel,
        out_shape=jax.ShapeDtypeStruct((t, d), F32),
        grid=(t // tm, d // tn),
        in_specs=[
            pl.BlockSpec((tm, tn), lambda i, j: (i, j)),
            pl.BlockSpec((tm, inner), lambda i, j: (i, 0)),
            pl.BlockSpec((tm, inner), lambda i, j: (i, 0)),
            pl.BlockSpec((1, inner), lambda i, j: (0, 0)),
            pl.BlockSpec((inner, tn), lambda i, j: (0, j)),
        ],
        out_specs=pl.BlockSpec((tm, tn), lambda i, j: (i, j)),
        scratch_shapes=[pltpu.VMEM((tm, inner), BF16)],
        compiler_params=_params("parallel", "arbitrary"),
        name="ssd_outproj",
    )(h, y, proj, g, w)


def _pad_cols(w, n):
    return jnp.pad(w, ((0, 0), (0, n - w.shape[1])))


def kernel(x, p, norm_mix, norm_mlp, ab_w_in, ab_w_gate_up, ab_b_gate, ab_gla_norm, ab_w_out,
           ssd_w_in, ssd_conv_w, ssd_conv_b, ssd_dt_bias, ssd_a_log, ssd_d, ssd_norm, ssd_w_out,
           mlp_w_up, mlp_w_down, ple_w_proj, ple_w_gate, final_norm):
    batch, seq, d = x.shape
    t = batch * seq
    depth = p.shape[0]
    tm = 512
    h = x.reshape(t, d)
    row = lambda v: v.reshape(1, -1).astype(F32)

    for layer in range(depth):
        j = layer // 2
        if layer % 2 == 0:
            w = ab_w_in[j]
            glr0 = 3 * SB_HEADS * SB_DIM + 2 * GLA_HEADS * GLA_DK + GLA_HEADS * GLA_DV
            w_main = jnp.concatenate([w[:, :glr0], w[:, glr0 + GLA_RANK:]], axis=1).astype(BF16)
            w_glr = _pad_cols(w[:, glr0:glr0 + GLA_RANK], LANES).astype(BF16)
            w_gu = jnp.pad(ab_w_gate_up[j], ((0, LANES - GLA_RANK), (0, 0))).astype(BF16)
            proj, log_a = _ab_inproj(h, row(norm_mix[layer]), w_main, w_glr, w_gu,
                                     row(ab_b_gate[j]), tm=tm, tn=1024)
            o_sb = _sb_attention(proj, batch=batch, seq=seq, bq=512, bk=256)
            o_gla = _gla(proj, log_a, row(ab_gla_norm[j]), batch=batch, seq=seq, ts=256)
            h = _ab_outproj(h, o_sb, o_gla, ab_w_out[j].astype(BF16), tm=tm)
        else:
            w = ssd_w_in[j]
            inner = ssd_w_out.shape[1]
            n_zx = w.shape[1] - inner // SSD_HEAD_DIM
            proj, dt_raw = _ssd_inproj(h, row(norm_mix[layer]), w[:, :n_zx].astype(BF16),
                                       _pad_cols(w[:, n_zx:], LANES).astype(BF16), tm=tm, tn=1024)
            y = _ssd(proj, dt_raw, ssd_conv_w[j], row(ssd_conv_b[j]),
                     _pad_cols(row(ssd_dt_bias[j]), LANES), _pad_cols(row(ssd_a_log[j]), LANES),
                     row(jnp.repeat(ssd_d[j], SSD_HEAD_DIM)), batch=batch, seq=seq, ts=256,
                     inner=inner)
            h = _ssd_outproj(h, y, proj, row(ssd_norm[j]), ssd_w_out[j].astype(BF16),
                             tm=256, tn=1024)
        h = _mlp(h, row(norm_mlp[layer]), mlp_w_up[layer].astype(BF16),
                 mlp_w_down[layer].astype(BF16), tm=tm, tf=512)
        h = _ple(h, p[layer].reshape(t, -1), ple_w_gate[layer].astype(BF16),
                 ple_w_proj[layer].astype(BF16), row(final_norm), tm=256,
                 final_norm=layer == depth - 1)
    return h.reshape(batch, seq, d)
```

```python
import functools

import numpy as np
import jax
import jax.numpy as jnp
from jax import lax
from jax.experimental import pallas as pl
from jax.experimental.pallas import tpu as pltpu

F32 = jnp.float32
BF16 = jnp.bfloat16

EPS = 1e-6
LOG2E = 1.4426950408889634
LANES = 128
VMEM_LIMIT = 56 * 1024 * 1024

SB_HEADS = 8
SB_DIM = 128
GLA_HEADS = 8
GLA_DK = 64
GLA_DV = 128
GLA_RANK = 16
GLA_TAU = 16.0
GLA_CHUNK = 64
SSD_HEAD_DIM = 64
SSD_GROUPS = 8
SSD_STATE = 128
SSD_CONV = 4
SSD_CHUNK = 128
CONV_HALO = 8


def _params(*sem):
    return pltpu.CompilerParams(dimension_semantics=sem, vmem_limit_bytes=VMEM_LIMIT)


def _dot(a, b):
    return jnp.dot(a, b, preferred_element_type=F32)


def _dot_nt(a, b):
    return lax.dot_general(a, b, (((1,), (1,)), ((), ())), preferred_element_type=F32)


def _dot_tn(a, b):
    return lax.dot_general(a, b, (((0,), (0,)), ((), ())), preferred_element_type=F32)


def _rms(x, g):
    ms = jnp.mean(x * x, axis=-1, keepdims=True)
    return x * lax.rsqrt(ms + EPS) * g


def _softplus_neg_abs(z):
    return jnp.log(1.0 + jnp.exp(-jnp.abs(z)))


def _log_sigmoid(z):
    return jnp.minimum(z, 0.0) - _softplus_neg_abs(z)


def _silu(z):
    return z * (1.0 / (1.0 + jnp.exp(-z)))


def _split3(x):
    hi = x.astype(BF16)
    r = x - hi.astype(F32)
    mid = r.astype(BF16)
    lo = (r - mid.astype(F32)).astype(BF16)
    return hi, mid, lo


def _dot_exact_rhs(m01, x):
    hi, mid, lo = _split3(x)
    return _dot(m01, hi) + _dot(m01, mid) + _dot(m01, lo)


def _dot_split_lhs(x, m01, pieces):
    out = None
    for _ in range(pieces):
        part = x.astype(BF16)
        x = x - part.astype(F32)
        term = _dot(part, m01)
        out = term if out is None else out + term
    return out


def _ab_inproj_kernel(x_ref, g_ref, w_ref, wglr_ref, wgu_ref, bg_ref, proj_ref, loga_ref, xn_ref):
    @pl.when(pl.program_id(1) == 0)
    def _():
        xb = _rms(x_ref[...], g_ref[...]).astype(BF16)
        xn_ref[...] = xb
        glr = _dot(xb, wglr_ref[...])
        ga = _dot(glr.astype(BF16), wgu_ref[...]) + bg_ref[...]
        loga_ref[...] = _log_sigmoid(ga) * (1.0 / GLA_TAU)

    proj_ref[...] = _dot(xn_ref[...], w_ref[...]).astype(BF16)


def _ab_inproj(x, g, w, wglr, wgu, bg, *, tm, tn):
    t, d = x.shape
    n = w.shape[1]
    gk = wgu.shape[1]
    return pl.pallas_call(
        _ab_inproj_kernel,
        out_shape=(jax.ShapeDtypeStruct((t, n), BF16), jax.ShapeDtypeStruct((t, gk), F32)),
        grid=(t // tm, n // tn),
        in_specs=[
            pl.BlockSpec((tm, d), lambda i, j: (i, 0)),
            pl.BlockSpec((1, d), lambda i, j: (0, 0)),
            pl.BlockSpec((d, tn), lambda i, j: (0, j)),
            pl.BlockSpec(wglr.shape, lambda i, j: (0, 0)),
            pl.BlockSpec(wgu.shape, lambda i, j: (0, 0)),
            pl.BlockSpec((1, gk), lambda i, j: (0, 0)),
        ],
        out_specs=(pl.BlockSpec((tm, tn), lambda i, j: (i, j)),
                   pl.BlockSpec((tm, gk), lambda i, j: (i, 0))),
        scratch_shapes=[pltpu.VMEM((tm, d), BF16)],
        compiler_params=_params("parallel", "arbitrary"),
        name="ab_inproj",
    )(x, g, w, wglr, wgu, bg)


def _ssd_inproj_kernel(x_ref, g_ref, w_ref, wdt_ref, proj_ref, dt_ref, xn_ref):
    @pl.when(pl.program_id(1) == 0)
    def _():
        xb = _rms(x_ref[...], g_ref[...]).astype(BF16)
        xn_ref[...] = xb
        dt_ref[...] = _dot(xb, wdt_ref[...])

    proj_ref[...] = _dot(xn_ref[...], w_ref[...]).astype(BF16)


def _ssd_inproj(x, g, w, wdt, *, tm, tn):
    t, d = x.shape
    n = w.shape[1]
    nd = wdt.shape[1]
    return pl.pallas_call(
        _ssd_inproj_kernel,
        out_shape=(jax.ShapeDtypeStruct((t, n), BF16), jax.ShapeDtypeStruct((t, nd), F32)),
        grid=(t // tm, n // tn),
        in_specs=[
            pl.BlockSpec((tm, d), lambda i, j: (i, 0)),
            pl.BlockSpec((1, d), lambda i, j: (0, 0)),
            pl.BlockSpec((d, tn), lambda i, j: (0, j)),
            pl.BlockSpec((d, nd), lambda i, j: (0, 0)),
        ],
        out_specs=(pl.BlockSpec((tm, tn), lambda i, j: (i, j)),
                   pl.BlockSpec((tm, nd), lambda i, j: (i, 0))),
        scratch_shapes=[pltpu.VMEM((tm, d), BF16)],
        compiler_params=_params("parallel", "arbitrary"),
        name="ssd_inproj",
    )(x, g, w, wdt)


def _sb_kernel(q_ref, k_ref, v_ref, o_ref, *, bq, bk, scale):
    i = pl.program_id(2)
    nd = bq // bk
    q = q_ref[...]
    key = lax.broadcasted_iota(jnp.int32, (bk, bq), 0)
    qry = lax.broadcasted_iota(jnp.int32, (bk, bq), 1)
    col = lax.broadcasted_iota(jnp.int32, (bk, 2 * bk), 1)
    row = lax.broadcasted_iota(jnp.int32, (bk, 2 * bk), 0)
    later2 = jnp.where(jnp.where(col >= bk, col - bk, col) > row, 1.0, 0.0).astype(BF16)

    def tile(kb, carry, acc, key_offset):
        start = pl.multiple_of(kb * bk, bk)
        kblk = k_ref[pl.ds(start, bk), :]
        vblk = v_ref[pl.ds(start, bk), :]
        z2 = _dot_nt(kblk, q) * (scale * LOG2E)
        pos = jnp.minimum(z2, 0.0)
        neg = pos - z2
        sp = jnp.log2(1.0 + jnp.exp2(neg + pos))
        log_beta = pos - sp
        l1 = neg - sp
        if key_offset is not None:
            causal = key + key_offset < qry
            l1 = jnp.where(causal, l1, 0.0)
        hi = l1.astype(BF16)
        lo = (l1 - hi.astype(F32)).astype(BF16)
        within = _dot(later2, jnp.concatenate([hi, lo], axis=0))
        w = jnp.exp2(log_beta + (within + carry))
        if key_offset is not None:
            w = jnp.where(causal, w, 0.0)
        acc = acc + _dot_tn(vblk, w.astype(BF16))
        carry = carry + (within[0:1, :] + l1[0:1, :])
        return carry, acc

    carry = jnp.zeros((1, bq), F32)
    acc = jnp.zeros((q.shape[1], bq), F32)
    for r in reversed(range(nd)):
        carry, acc = tile(nd * i + r, carry, acc, r * bk)

    def body(n, c):
        for r in reversed(range(nd)):
            c = tile(nd * (i - 1 - n) + r, c[0], c[1], None)
        return c

    carry, acc = lax.fori_loop(0, i, body, (carry, acc))
    o_ref[...] = acc.T.astype(BF16)


def _sb_attention(proj, *, batch, seq, bq, bk):
    t = proj.shape[0]
    nq = seq // bq
    kernel = functools.partial(_sb_kernel, bq=bq, bk=bk, scale=SB_DIM ** -0.5)
    return pl.pallas_call(
        kernel,
        out_shape=jax.ShapeDtypeStruct((t, SB_HEADS * SB_DIM), BF16),
        grid=(batch, SB_HEADS, nq),
        in_specs=[
            pl.BlockSpec((bq, SB_DIM), lambda b, h, i: (b * nq + i, h)),
            pl.BlockSpec((seq, SB_DIM), lambda b, h, i: (b, SB_HEADS + h)),
            pl.BlockSpec((seq, SB_DIM), lambda b, h, i: (b, 2 * SB_HEADS + h)),
        ],
        out_specs=pl.BlockSpec((bq, SB_DIM), lambda b, h, i: (b * nq + i, h)),
        compiler_params=_params("parallel", "parallel", "arbitrary"),
        name="sb_attention",
    )(proj, proj, proj)


def _gla_kernel(q_ref, k_ref, v_ref, go_ref, la_ref, gn_ref, o_ref, st_ref, *, nchunks):
    c = GLA_CHUNK

    @pl.when(pl.program_id(2) == 0)
    def _():
        st_ref[...] = jnp.zeros_like(st_ref)

    row = lax.broadcasted_iota(jnp.int32, (c, c), 0)
    col = lax.broadcasted_iota(jnp.int32, (c, c), 1)
    incl = jnp.where(col <= row, 1.0, 0.0).astype(BF16)
    causal = col <= row
    lane = lax.broadcasted_iota(jnp.int32, (1, 2 * GLA_DK), 1)
    head_lanes = (lane < GLA_DK, lane >= GLA_DK)
    srow = lax.broadcasted_iota(jnp.int32, (2 * GLA_DV, 2 * GLA_DK), 0)
    scol = lax.broadcasted_iota(jnp.int32, (2 * GLA_DV, 2 * GLA_DK), 1)
    same_head = (srow < GLA_DV) == (scol < GLA_DK)
    gn = gn_ref[...]

    for n in range(nchunks):
        sl = pl.ds(n * c, c)
        gcum = _dot_exact_rhs(incl, la_ref[sl, :])
        glast = gcum[c - 1:c, :]
        q = q_ref[sl, :].astype(F32) * (GLA_DK ** -0.5)
        k = k_ref[sl, :].astype(F32)
        v = v_ref[sl, :]
        q_dec = (q * jnp.exp(gcum)).astype(BF16)
        k_inv = (k * jnp.exp(-gcum)).astype(BF16)
        k_end = (k * jnp.exp(glast - gcum)).astype(BF16)
        st = st_ref[...]
        o_inter = _dot_nt(q_dec, st.astype(BF16))
        o_intra = []
        for a in range(2):
            qa = jnp.where(head_lanes[a], q_dec, jnp.zeros_like(q_dec))
            scores = jnp.where(causal, _dot_nt(qa, k_inv), 0.0)
            o_intra.append(_dot(scores.astype(BF16), v[:, a * GLA_DV:(a + 1) * GLA_DV]))
        st_ref[...] = st * jnp.exp(glast) + jnp.where(same_head, _dot_tn(v, k_end), 0.0)
        gate = _silu(go_ref[sl, :].astype(F32))
        outs = []
        for a in range(2):
            hs = slice(a * GLA_DV, (a + 1) * GLA_DV)
            o = o_intra[a] + o_inter[:, hs]
            outs.append(_rms(o, gn) * gate[:, hs])
        o_ref[sl, :] = jnp.concatenate(outs, axis=1).astype(BF16)


def _gla(proj, log_a, gnorm, *, batch, seq, ts):
    t = proj.shape[0]
    nt = seq // ts
    pairs = GLA_HEADS // 2
    kw, vw = 2 * GLA_DK, 2 * GLA_DV
    q0 = 3 * SB_HEADS * SB_DIM // kw
    k0 = q0 + GLA_HEADS * GLA_DK // kw
    v0 = (3 * SB_HEADS * SB_DIM + 2 * GLA_HEADS * GLA_DK) // vw
    g0 = v0 + GLA_HEADS * GLA_DV // vw
    kernel = functools.partial(_gla_kernel, nchunks=ts // GLA_CHUNK)
    return pl.pallas_call(
        kernel,
        out_shape=jax.ShapeDtypeStruct((t, GLA_HEADS * GLA_DV), BF16),
        grid=(batch, pairs, nt),
        in_specs=[
            pl.BlockSpec((ts, kw), lambda b, p, i: (b * nt + i, q0 + p)),
            pl.BlockSpec((ts, kw), lambda b, p, i: (b * nt + i, k0 + p)),
            pl.BlockSpec((ts, vw), lambda b, p, i: (b * nt + i, v0 + p)),
            pl.BlockSpec((ts, vw), lambda b, p, i: (b * nt + i, g0 + p)),
            pl.BlockSpec((ts, kw), lambda b, p, i: (b * nt + i, p)),
            pl.BlockSpec((1, GLA_DV), lambda b, p, i: (0, 0)),
        ],
        out_specs=pl.BlockSpec((ts, vw), lambda b, p, i: (b * nt + i, p)),
        scratch_shapes=[pltpu.VMEM((vw, kw), F32)],
        compiler_params=_params("parallel", "parallel", "arbitrary"),
        name="gla",
    )(proj, proj, proj, proj, log_a, gnorm)


def _ab_outproj_kernel(h_ref, a_ref, b_ref, w_ref, o_ref):
    ka = a_ref.shape[1]
    acc = _dot(a_ref[...], w_ref[0:ka, :]) + _dot(b_ref[...], w_ref[ka:, :])
    o_ref[...] = h_ref[...] + acc


def _ab_outproj(h, a, b, w, *, tm):
    t, d = h.shape
    return pl.pallas_call(
        _ab_outproj_kernel,
        out_shape=jax.ShapeDtypeStruct((t, d), F32),
        grid=(t // tm,),
        in_specs=[
            pl.BlockSpec((tm, d), lambda i: (i, 0)),
            pl.BlockSpec((tm, a.shape[1]), lambda i: (i, 0)),
            pl.BlockSpec((tm, b.shape[1]), lambda i: (i, 0)),
            pl.BlockSpec(w.shape, lambda i: (0, 0)),
        ],
        out_specs=pl.BlockSpec((tm, d), lambda i: (i, 0)),
        compiler_params=_params("parallel"),
        name="ab_outproj",
    )(h, a, b, w)


def _mlp_kernel(h_ref, g_ref, wu_ref, wd_ref, o_ref, xn_ref):
    @pl.when(pl.program_id(1) == 0)
    def _():
        h = h_ref[...]
        xn_ref[...] = _rms(h, g_ref[...]).astype(BF16)
        o_ref[...] = h

    u = jnp.maximum(_dot(xn_ref[...], wu_ref[...]), 0.0)
    o_ref[...] += _dot((u * u).astype(BF16), wd_ref[...])


def _mlp(h, g, wu, wd, *, tm, tf):
    t, d = h.shape
    f = wu.shape[1]
    return pl.pallas_call(
        _mlp_kernel,
        out_shape=jax.ShapeDtypeStruct((t, d), F32),
        grid=(t // tm, f // tf),
        in_specs=[
            pl.BlockSpec((tm, d), lambda i, j: (i, 0)),
            pl.BlockSpec((1, d), lambda i, j: (0, 0)),
            pl.BlockSpec((d, tf), lambda i, j: (0, j)),
            pl.BlockSpec((tf, d), lambda i, j: (j, 0)),
        ],
        out_specs=pl.BlockSpec((tm, d), lambda i, j: (i, 0)),
        scratch_shapes=[pltpu.VMEM((tm, d), BF16)],
        compiler_params=_params("parallel", "arbitrary"),
        name="mlp",
    )(h, g, wu, wd)


def _ple_kernel(h_ref, p_ref, wg_ref, wp_ref, fn_ref, o_ref, *, final_norm):
    h = h_ref[...]
    gate = _dot(h.astype(BF16), wg_ref[...])
    emb = _dot(p_ref[...].astype(BF16), wp_ref[...])
    out = h + emb * (1.0 / (1.0 + jnp.exp(-gate)))
    if final_norm:
        out = _rms(out, fn_ref[...])
    o_ref[...] = out


def _ple(h, p, wg, wp, fn, *, tm, final_norm):
    t, d = h.shape
    kernel = functools.partial(_ple_kernel, final_norm=final_norm)
    return pl.pallas_call(
        kernel,
        out_shape=jax.ShapeDtypeStruct((t, d), F32),
        grid=(t // tm,),
        in_specs=[
            pl.BlockSpec((tm, d), lambda i: (i, 0)),
            pl.BlockSpec((tm, p.shape[1]), lambda i: (i, 0)),
            pl.BlockSpec(wg.shape, lambda i: (0, 0)),
            pl.BlockSpec(wp.shape, lambda i: (0, 0)),
            pl.BlockSpec((1, d), lambda i: (0, 0)),
        ],
        out_specs=pl.BlockSpec((tm, d), lambda i: (i, 0)),
        compiler_params=_params("parallel"),
        name="ple",
    )(h, p, wg, wp, fn)


def _ssd_kernel(xbc_ref, par_ref, dt_ref, sel_ref, y_ref, st_ref, pad_ref, *, nchunks, heads):
    c = SSD_CHUNK
    ts = xbc_ref.shape[0]
    gw = heads * SSD_HEAD_DIM
    n = SSD_STATE
    first = pl.program_id(2) == 0

    @pl.when(first)
    def _():
        st_ref[...] = jnp.zeros_like(st_ref)
        pad_ref[0:CONV_HALO, :] = jnp.zeros((CONV_HALO, pad_ref.shape[1]), F32)

    pad_ref[CONV_HALO:CONV_HALO + ts, :] = xbc_ref[...].astype(F32)
    conv = par_ref[0, 0:1, :] + par_ref[0, SSD_CONV:SSD_CONV + 1, :] * pad_ref[CONV_HALO:CONV_HALO + ts, :]
    for back in range(1, SSD_CONV):
        tap = SSD_CONV - back
        conv = conv + par_ref[0, tap:tap + 1, :] * pad_ref[CONV_HALO - back:CONV_HALO - back + ts, :]
    pad_ref[0:CONV_HALO, :] = pad_ref[ts:ts + CONV_HALO, :]
    xbc = _silu(conv)
    xs_all = xbc[:, :gw]
    bm_all = xbc[:, gw:gw + n].astype(BF16)
    cm_all = xbc[:, gw + n:].astype(BF16)

    row = lax.broadcasted_iota(jnp.int32, (c, c), 0)
    col = lax.broadcasted_iota(jnp.int32, (c, c), 1)
    tril = col <= row
    incl = jnp.where(tril, 1.0, 0.0).astype(BF16)
    lane = lax.broadcasted_iota(jnp.int32, (1, LANES), 1)
    lo_half = lane < SSD_HEAD_DIM
    sel = sel_ref[0]
    dsk = par_ref[0, 5:6, 0:gw]
    dt_bias = par_ref[0, 6:7, 0:LANES]
    a2 = -jnp.exp(par_ref[0, 7:8, 0:LANES]) * LOG2E

    for ci in range(nchunks):
        sl = slice(ci * c, (ci + 1) * c)
        xs = xs_all[sl, :]
        bm = bm_all[sl, :]
        cm = cm_all[sl, :]
        z = _dot_split_lhs(dt_ref[sl, :], sel, 2) + dt_bias
        dt = jnp.maximum(z, 0.0) + _softplus_neg_abs(z)
        a_cs = _dot_exact_rhs(incl, dt * a2)
        a_cs_t = a_cs.T
        a_last = a_cs[c - 1:c, :]
        cb = _dot_nt(cm, bm)
        y_pairs, ea_pairs, xds_pairs = [], [], []
        for pr in range(heads // 2):
            cols, ms = [], []
            for j in (2 * pr, 2 * pr + 1):
                a_col = jnp.broadcast_to(a_cs[:, j:j + 1], (c, LANES))
                decay = jnp.where(tril, jnp.exp2(a_col - a_cs_t[j:j + 1, :]), 0.0)
                ms.append((cb * decay).astype(BF16))
                cols.append((a_col, jnp.broadcast_to(dt[:, j:j + 1], (c, LANES)), a_last[:, j:j + 1]))
            pick = lambda f: jnp.where(lo_half, f(cols[0]), f(cols[1]))
            xp = xs[:, pr * LANES:(pr + 1) * LANES] * pick(lambda v: v[1])
            ea_pairs.append(pick(lambda v: jnp.exp2(v[0])))
            xds_pairs.append(xp * pick(lambda v: jnp.exp2(v[2] - v[0])))
            lhs = jnp.concatenate(ms, axis=1)
            rhs = jnp.concatenate([jnp.where(lo_half, xp, 0.0), jnp.where(lo_half, 0.0, xp)],
                                  axis=0).astype(BF16)
            y_pairs.append(_dot(lhs, rhs))
        y_diag = jnp.concatenate(y_pairs, axis=1)
        ea_x = jnp.concatenate(ea_pairs, axis=1)
        st = st_ref[...]
        y_off = _dot(cm, st.astype(BF16)) * ea_x
        st_ref[...] = st * ea_x[c - 1:c, :] + _dot_tn(bm, jnp.concatenate(xds_pairs, axis=1).astype(BF16))
        y_ref[sl, :] = (y_diag + y_off + dsk * xs).astype(BF16)


def _ssd(proj, dt_raw, params, *, batch, seq, ts, inner):
    t = proj.shape[0]
    nt = seq // ts
    g = SSD_GROUPS
    gw = inner // g
    heads = gw // SSD_HEAD_DIM
    n = SSD_STATE
    cw = gw + 2 * n

    sel = np.zeros((g, LANES, LANES), np.float32)
    for j in range(heads):
        for gi in range(g):
            sel[gi, gi * heads + j, j] = 1.0
    sel = jnp.asarray(sel, BF16)

    kernel = functools.partial(_ssd_kernel, nchunks=ts // SSD_CHUNK, heads=heads)
    row = lambda b, gi, i: b * nt + i
    return pl.pallas_call(
        kernel,
        out_shape=jax.ShapeDtypeStruct((t, inner), BF16),
        grid=(batch, g, nt),
        in_specs=[
            pl.BlockSpec((ts, cw), lambda b, gi, i: (row(b, gi, i), gi)),
            pl.BlockSpec((1, 8, cw), lambda b, gi, i: (gi, 0, 0)),
            pl.BlockSpec((ts, LANES), lambda b, gi, i: (row(b, gi, i), 0)),
            pl.BlockSpec((1, LANES, LANES), lambda b, gi, i: (gi, 0, 0)),
        ],
        out_specs=pl.BlockSpec((ts, gw), lambda b, gi, i: (row(b, gi, i), gi)),
        scratch_shapes=[
            pltpu.VMEM((n, gw), F32),
            pltpu.VMEM((ts + CONV_HALO, cw), F32),
        ],
        compiler_params=_params("parallel", "parallel", "arbitrary"),
        name="ssd",
    )(proj, params, dt_raw, sel)


def _ssd_outproj_kernel(h_ref, y_ref, zlo_ref, zhi_ref, g_ref, w_ref, o_ref, *, kc):
    inner = y_ref.shape[1]
    half = zlo_ref.shape[1]
    ssq = None
    acc = None
    for k0 in range(0, inner, kc):
        z_ref, zoff = (zlo_ref, k0) if k0 < half else (zhi_ref, k0 - half)
        z = z_ref[:, zoff:zoff + kc].astype(F32)
        yg = y_ref[:, k0:k0 + kc].astype(F32) * _silu(z)
        part = jnp.sum(yg * yg, axis=-1, keepdims=True)
        ssq = part if ssq is None else ssq + part
        term = _dot((yg * g_ref[:, k0:k0 + kc]).astype(BF16), w_ref[k0:k0 + kc, :])
        acc = term if acc is None else acc + term
    o_ref[...] = h_ref[...] + acc * lax.rsqrt(ssq * (1.0 / inner) + EPS)


def _ssd_outproj(h, y, proj, g, w, *, tm, kc):
    t, d = h.shape
    inner = y.shape[1]
    half = inner // 2
    z0 = (proj.shape[1] - inner) // half
    kernel = functools.partial(_ssd_outproj_kernel, kc=kc)
    return pl.pallas_call(
        kernel,
        out_shape=jax.ShapeDtypeStruct((t, d), F32),
        grid=(t // tm,),
        in_specs=[
            pl.BlockSpec((tm, d), lambda i: (i, 0)),
            pl.BlockSpec((tm, inner), lambda i: (i, 0)),
            pl.BlockSpec((tm, half), lambda i: (i, z0)),
            pl.BlockSpec((tm, half), lambda i: (i, z0 + 1)),
            pl.BlockSpec((1, inner), lambda i: (0, 0)),
            pl.BlockSpec((inner, d), lambda i: (0, 0), pipeline_mode=pl.Buffered(1)),
        ],
        out_specs=pl.BlockSpec((tm, d), lambda i: (i, 0)),
        compiler_params=_params("parallel"),
        name="ssd_outproj",
    )(h, y, proj, proj, g, w)


def _pad_cols(w, n):
    return jnp.pad(w, ((0, 0), (0, n - w.shape[1])))


def _group_cols(a, inner):
    g, n = SSD_GROUPS, SSD_STATE
    gw = inner // g
    b0, c0 = inner, inner + g * n
    out = []
    for gi in range(g):
        out += [a[:, gi * gw:(gi + 1) * gw], a[:, b0 + gi * n:b0 + (gi + 1) * n],
                a[:, c0 + gi * n:c0 + (gi + 1) * n]]
    return out


def _ssd_params(conv_w, conv_b, d_skip, dt_bias, a_log, inner):
    g = SSD_GROUPS
    cw = inner // g + 2 * SSD_STATE
    grouped = jnp.concatenate(_group_cols(jnp.concatenate([conv_b[None, :], conv_w], axis=0), inner),
                              axis=1)
    grouped = grouped.reshape(1 + SSD_CONV, g, cw).transpose(1, 0, 2)
    per_head = lambda v, rep: _pad_cols(jnp.repeat(v, rep).reshape(g, -1), cw)[:, None, :]
    return jnp.concatenate([grouped, per_head(d_skip, SSD_HEAD_DIM), per_head(dt_bias, 1),
                            per_head(a_log, 1)], axis=1).astype(F32)


def kernel(x, p, norm_mix, norm_mlp, ab_w_in, ab_w_gate_up, ab_b_gate, ab_gla_norm, ab_w_out,
           ssd_w_in, ssd_conv_w, ssd_conv_b, ssd_dt_bias, ssd_a_log, ssd_d, ssd_norm, ssd_w_out,
           mlp_w_up, mlp_w_down, ple_w_proj, ple_w_gate, final_norm):
    batch, seq, d = x.shape
    t = batch * seq
    depth = p.shape[0]
    tm = 512
    h = x.reshape(t, d)
    row = lambda v: v.reshape(1, -1).astype(F32)

    for layer in range(depth):
        j = layer // 2
        if layer % 2 == 0:
            w = ab_w_in[j]
            glr0 = 3 * SB_HEADS * SB_DIM + 2 * GLA_HEADS * GLA_DK + GLA_HEADS * GLA_DV
            w_main = jnp.concatenate([w[:, :glr0], w[:, glr0 + GLA_RANK:]], axis=1).astype(BF16)
            w_glr = _pad_cols(w[:, glr0:glr0 + GLA_RANK], LANES).astype(BF16)
            w_gu = jnp.pad(ab_w_gate_up[j], ((0, LANES - GLA_RANK), (0, 0))).astype(BF16)
            proj, log_a = _ab_inproj(h, row(norm_mix[layer]), w_main, w_glr, w_gu,
                                     row(ab_b_gate[j]), tm=tm, tn=1024)
            o_sb = _sb_attention(proj, batch=batch, seq=seq, bq=512, bk=256)
            o_gla = _gla(proj, log_a, row(ab_gla_norm[j]), batch=batch, seq=seq, ts=256)
            h = _ab_outproj(h, o_sb, o_gla, ab_w_out[j].astype(BF16), tm=tm)
        else:
            w = ssd_w_in[j]
            inner = ssd_w_out.shape[1]
            n_zx = w.shape[1] - inner // SSD_HEAD_DIM
            w_perm = jnp.concatenate(_group_cols(w[:, inner:n_zx], inner) + [w[:, :inner]],
                                     axis=1).astype(BF16)
            proj, dt_raw = _ssd_inproj(h, row(norm_mix[layer]), w_perm,
                                       _pad_cols(w[:, n_zx:], LANES).astype(BF16), tm=tm, tn=1024)
            params = _ssd_params(ssd_conv_w[j], ssd_conv_b[j], ssd_d[j], ssd_dt_bias[j],
                                 ssd_a_log[j], inner)
            y = _ssd(proj, dt_raw, params, batch=batch, seq=seq, ts=512, inner=inner)
            h = _ssd_outproj(h, y, proj, row(ssd_norm[j]), ssd_w_out[j].astype(BF16),
                             tm=256, kc=1024)
        h = _mlp(h, row(norm_mlp[layer]), mlp_w_up[layer].astype(BF16),
                 mlp_w_down[layer].astype(BF16), tm=tm, tf=512)
        h = _ple(h, p[layer].reshape(t, -1), ple_w_gate[layer].astype(BF16),
                 ple_w_proj[layer].astype(BF16), row(final_norm), tm=256,
                 final_norm=layer == depth - 1)
    return h.reshape(batch, seq, d)
```

```python
import functools

import numpy as np
import jax
import jax.numpy as jnp
from jax import lax
from jax.experimental import pallas as pl
from jax.experimental.pallas import tpu as pltpu

F32 = jnp.float32
BF16 = jnp.bfloat16

EPS = 1e-6
LOG2E = 1.4426950408889634
LANES = 128
VMEM_LIMIT = 56 * 1024 * 1024

SB_HEADS = 8
SB_DIM = 128
GLA_HEADS = 8
GLA_DK = 64
GLA_DV = 128
GLA_RANK = 16
GLA_TAU = 16.0
GLA_CHUNK = 64
SSD_HEAD_DIM = 64
SSD_GROUPS = 8
SSD_STATE = 128
SSD_CONV = 4
SSD_CHUNK = 128
CONV_HALO = 8


def _params(*sem):
    return pltpu.CompilerParams(dimension_semantics=sem, vmem_limit_bytes=VMEM_LIMIT)


def _dot(a, b):
    return jnp.dot(a, b, preferred_element_type=F32)


def _dot_nt(a, b):
    return lax.dot_general(a, b, (((1,), (1,)), ((), ())), preferred_element_type=F32)


def _dot_tn(a, b):
    return lax.dot_general(a, b, (((0,), (0,)), ((), ())), preferred_element_type=F32)


def _rms(x, g):
    ms = jnp.mean(x * x, axis=-1, keepdims=True)
    return x * lax.rsqrt(ms + EPS) * g


def _softplus_neg_abs(z):
    return jnp.log(1.0 + jnp.exp(-jnp.abs(z)))


def _log_sigmoid(z):
    return jnp.minimum(z, 0.0) - _softplus_neg_abs(z)


def _silu(z):
    return z * (1.0 / (1.0 + jnp.exp(-z)))


def _split3(x):
    hi = x.astype(BF16)
    r = x - hi.astype(F32)
    mid = r.astype(BF16)
    lo = (r - mid.astype(F32)).astype(BF16)
    return hi, mid, lo


def _dot_exact_rhs(m01, x):
    hi, mid, lo = _split3(x)
    return _dot(m01, hi) + _dot(m01, mid) + _dot(m01, lo)


def _dot_split_lhs(x, m01, pieces):
    out = None
    for _ in range(pieces):
        part = x.astype(BF16)
        x = x - part.astype(F32)
        term = _dot(part, m01)
        out = term if out is None else out + term
    return out


def _ab_inproj_kernel(x_ref, g_ref, w_ref, wglr_ref, wgu_ref, bg_ref, proj_ref, loga_ref, xn_ref):
    @pl.when(pl.program_id(1) == 0)
    def _():
        xb = _rms(x_ref[...], g_ref[...]).astype(BF16)
        xn_ref[...] = xb
        glr = _dot(xb, wglr_ref[...])
        ga = _dot(glr.astype(BF16), wgu_ref[...]) + bg_ref[...]
        loga_ref[...] = _log_sigmoid(ga) * (1.0 / GLA_TAU)

    proj_ref[...] = _dot(xn_ref[...], w_ref[...]).astype(BF16)


def _ab_inproj(x, g, w, wglr, wgu, bg, *, tm, tn):
    t, d = x.shape
    n = w.shape[1]
    gk = wgu.shape[1]
    return pl.pallas_call(
        _ab_inproj_kernel,
        out_shape=(jax.ShapeDtypeStruct((t, n), BF16), jax.ShapeDtypeStruct((t, gk), F32)),
        grid=(t // tm, n // tn),
        in_specs=[
            pl.BlockSpec((tm, d), lambda i, j: (i, 0)),
            pl.BlockSpec((1, d), lambda i, j: (0, 0)),
            pl.BlockSpec((d, tn), lambda i, j: (0, j)),
            pl.BlockSpec(wglr.shape, lambda i, j: (0, 0)),
            pl.BlockSpec(wgu.shape, lambda i, j: (0, 0)),
            pl.BlockSpec((1, gk), lambda i, j: (0, 0)),
        ],
        out_specs=(pl.BlockSpec((tm, tn), lambda i, j: (i, j)),
                   pl.BlockSpec((tm, gk), lambda i, j: (i, 0))),
        scratch_shapes=[pltpu.VMEM((tm, d), BF16)],
        compiler_params=_params("parallel", "arbitrary"),
        name="ab_inproj",
    )(x, g, w, wglr, wgu, bg)


def _ssd_inproj_kernel(x_ref, g_ref, w_ref, wdt_ref, proj_ref, dt_ref, xn_ref):
    @pl.when(pl.program_id(1) == 0)
    def _():
        xb = _rms(x_ref[...], g_ref[...]).astype(BF16)
        xn_ref[...] = xb
        dt_ref[...] = _dot(xb, wdt_ref[...])

    proj_ref[...] = _dot(xn_ref[...], w_ref[...]).astype(BF16)


def _ssd_inproj(x, g, w, wdt, *, tm, tn):
    t, d = x.shape
    n = w.shape[1]
    nd = wdt.shape[1]
    return pl.pallas_call(
        _ssd_inproj_kernel,
        out_shape=(jax.ShapeDtypeStruct((t, n), BF16), jax.ShapeDtypeStruct((t, nd), F32)),
        grid=(t // tm, n // tn),
        in_specs=[
            pl.BlockSpec((tm, d), lambda i, j: (i, 0)),
            pl.BlockSpec((1, d), lambda i, j: (0, 0)),
            pl.BlockSpec((d, tn), lambda i, j: (0, j)),
            pl.BlockSpec((d, nd), lambda i, j: (0, 0)),
        ],
        out_specs=(pl.BlockSpec((tm, tn), lambda i, j: (i, j)),
                   pl.BlockSpec((tm, nd), lambda i, j: (i, 0))),
        scratch_shapes=[pltpu.VMEM((tm, d), BF16)],
        compiler_params=_params("parallel", "arbitrary"),
        name="ssd_inproj",
    )(x, g, w, wdt)


SB_STAGES = 5
SB_RING = 4
SB_MASKED = -1e30


def _sb_kernel(tab_ref, q_ref, k_ref, v_ref, o_ref, z_buf, lb_buf, hl_buf, wi_buf, p_buf, acc_ref,
               carry_ref, *, bq, bk, npairs, scale):
    hk = bk // 2
    j = lax.broadcasted_iota(jnp.int32, (bk, bk), 0)
    c = lax.broadcasted_iota(jnp.int32, (bk, bk), 1)
    key_of_row = jnp.where(j >= hk, j - hk, j)
    suffix = jnp.where((key_of_row > c) | (c >= hk), 1.0, 0.0).astype(BF16)
    diff = (lax.broadcasted_iota(jnp.int32, (bq, bk), 1) -
            lax.broadcasted_iota(jnp.int32, (bq, bk), 0))

    def desc(u):
        return tab_ref[0, u], tab_ref[1, u], tab_ref[2, u], tab_ref[3, u]

    def s1(u):
        i, k0, _, _ = desc(u)
        q = q_ref[pl.ds(pl.multiple_of(i * bq, bq), bq), :]
        kk = k_ref[pl.ds(pl.multiple_of(k0, bk), bk), :]
        z_buf[u & (SB_RING - 1)] = _dot_nt(q, kk)

    def s2(u):
        _, _, off, _ = desc(u)
        slot = u & (SB_RING - 1)
        z2 = jnp.where(diff < off, z_buf[slot] * (scale * LOG2E), SB_MASKED)
        pos = jnp.minimum(z2, 0.0)
        neg = pos - z2
        sp = jnp.log2(1.0 + jnp.exp2(neg + pos))
        lb_buf[slot] = pos - sp
        l1 = neg - sp
        top = lax.bitcast_convert_type(lax.bitcast_convert_type(l1, jnp.int32) & -65536, F32)
        hi = top.astype(BF16)
        lo = (l1 - top).astype(BF16)
        hl_buf[slot, 0] = jnp.concatenate([hi[:, hk:], lo[:, hk:]], axis=1)
        hl_buf[slot, 1] = jnp.concatenate([hi[:, :hk], lo[:, :hk]], axis=1)

    def s3(u):
        slot = u & (SB_RING - 1)
        wi_buf[slot, 0] = _dot(hl_buf[slot, 0], suffix)
        wi_buf[slot, 1] = _dot(hl_buf[slot, 1], suffix)

    def s4(u):
        _, _, _, last = desc(u)
        slot = u & (SB_RING - 1)
        wa = wi_buf[slot, 0]
        wb = wi_buf[slot, 1]
        carry = carry_ref[...]
        after_a = wa[:, :hk] + carry
        carry = carry + wa[:, hk:]
        after_b = wb[:, :hk] + carry
        carry = carry + wb[:, hk:]
        after = jnp.concatenate([after_b, after_a], axis=1)
        p_buf[slot] = jnp.exp2(lb_buf[slot] + after).astype(BF16)
        carry_ref[...] = jnp.where(last == 1, 0.0, carry)

    def s5(u):
        _, k0, _, _ = desc(u)
        vv = v_ref[pl.ds(pl.multiple_of(k0, bk), bk), :]
        acc_ref[...] += _dot(p_buf[u & (SB_RING - 1)], vv)

    def flush(u):
        i, _, _, last = desc(u)

        @pl.when(last == 1)
        def _():
            o_ref[pl.ds(pl.multiple_of(i * bq, bq), bq), :] = acc_ref[...].astype(BF16)
            acc_ref[...] = jnp.zeros_like(acc_ref)

    def iteration(u, live):
        if live(u - 4):
            s5(u - 4)
        if live(u - 3):
            s4(u - 3)
        if live(u - 2):
            s3(u - 2)
        if live(u - 1):
            s2(u - 1)
        if live(u):
            s1(u)
        if live(u - 4):
            flush(u - 4)

    acc_ref[...] = jnp.zeros_like(acc_ref)
    carry_ref[...] = jnp.zeros_like(carry_ref)
    fill = SB_STAGES - 1
    for u in range(fill):
        iteration(u, lambda pair: 0 <= pair)

    def body(u, _):
        iteration(u, lambda pair: True)
        return 0

    lax.fori_loop(fill, npairs, body, 0)
    for u in range(npairs, npairs + fill):
        iteration(u, lambda pair: pair < npairs)


def _sb_attention(proj, *, batch, seq, bq, bk):
    t = proj.shape[0]
    hk = bk // 2
    tab = []
    for i in range(seq // bq):
        for k0 in range((i + 1) * bq - bk, -1, -bk):
            tab.append((i, k0, i * bq - k0, int(k0 == 0)))
    npairs = len(tab)
    tab = jnp.asarray(np.array(tab, np.int32).T)
    kernel = functools.partial(_sb_kernel, bq=bq, bk=bk, npairs=npairs, scale=SB_DIM ** -0.5)
    return pl.pallas_call(
        kernel,
        out_shape=jax.ShapeDtypeStruct((t, SB_HEADS * SB_DIM), BF16),
        grid=(batch, SB_HEADS),
        in_specs=[
            pl.BlockSpec(memory_space=pltpu.SMEM),
            pl.BlockSpec((seq, SB_DIM), lambda b, h: (b, h)),
            pl.BlockSpec((seq, SB_DIM), lambda b, h: (b, SB_HEADS + h)),
            pl.BlockSpec((seq, SB_DIM), lambda b, h: (b, 2 * SB_HEADS + h)),
        ],
        out_specs=pl.BlockSpec((seq, SB_DIM), lambda b, h: (b, h)),
        scratch_shapes=[
            pltpu.VMEM((SB_RING, bq, bk), F32),
            pltpu.VMEM((SB_RING, bq, bk), F32),
            pltpu.VMEM((SB_RING, 2, bq, bk), BF16),
            pltpu.VMEM((SB_RING, 2, bq, bk), F32),
            pltpu.VMEM((SB_RING, bq, bk), BF16),
            pltpu.VMEM((bq, SB_DIM), F32),
            pltpu.VMEM((bq, hk), F32),
        ],
        compiler_params=_params("parallel", "parallel"),
        name="sb_attention",
    )(tab, proj, proj, proj)


def _gla_kernel(q_ref, k_ref, v_ref, go_ref, la_ref, gn_ref, o_ref, st_ref, *, nchunks):
    c = GLA_CHUNK

    @pl.when(pl.program_id(2) == 0)
    def _():
        st_ref[...] = jnp.zeros_like(st_ref)

    row = lax.broadcasted_iota(jnp.int32, (c, c), 0)
    col = lax.broadcasted_iota(jnp.int32, (c, c), 1)
    incl = jnp.where(col <= row, 1.0, 0.0).astype(BF16)
    causal = col <= row
    lane = lax.broadcasted_iota(jnp.int32, (1, 2 * GLA_DK), 1)
    head_lanes = (lane < GLA_DK, lane >= GLA_DK)
    srow = lax.broadcasted_iota(jnp.int32, (2 * GLA_DV, 2 * GLA_DK), 0)
    scol = lax.broadcasted_iota(jnp.int32, (2 * GLA_DV, 2 * GLA_DK), 1)
    same_head = (srow < GLA_DV) == (scol < GLA_DK)
    gn = gn_ref[...]

    for n in range(nchunks):
        sl = pl.ds(n * c, c)
        gcum = _dot_exact_rhs(incl, la_ref[sl, :])
        glast = gcum[c - 1:c, :]
        q = q_ref[sl, :].astype(F32) * (GLA_DK ** -0.5)
        k = k_ref[sl, :].astype(F32)
        v = v_ref[sl, :]
        q_dec = (q * jnp.exp(gcum)).astype(BF16)
        k_inv = (k * jnp.exp(-gcum)).astype(BF16)
        k_end = (k * jnp.exp(glast - gcum)).astype(BF16)
        st = st_ref[...]
        o_inter = _dot_nt(q_dec, st.astype(BF16))
        o_intra = []
        for a in range(2):
            qa = jnp.where(head_lanes[a], q_dec, jnp.zeros_like(q_dec))
            scores = jnp.where(causal, _dot_nt(qa, k_inv), 0.0)
            o_intra.append(_dot(scores.astype(BF16), v[:, a * GLA_DV:(a + 1) * GLA_DV]))
        st_ref[...] = st * jnp.exp(glast) + jnp.where(same_head, _dot_tn(v, k_end), 0.0)
        gate = _silu(go_ref[sl, :].astype(F32))
        outs = []
        for a in range(2):
            hs = slice(a * GLA_DV, (a + 1) * GLA_DV)
            o = o_intra[a] + o_inter[:, hs]
            outs.append(_rms(o, gn) * gate[:, hs])
        o_ref[sl, :] = jnp.concatenate(outs, axis=1).astype(BF16)


def _gla(proj, log_a, gnorm, *, batch, seq, ts):
    t = proj.shape[0]
    nt = seq // ts
    pairs = GLA_HEADS // 2
    kw, vw = 2 * GLA_DK, 2 * GLA_DV
    q0 = 3 * SB_HEADS * SB_DIM // kw
    k0 = q0 + GLA_HEADS * GLA_DK // kw
    v0 = (3 * SB_HEADS * SB_DIM + 2 * GLA_HEADS * GLA_DK) // vw
    g0 = v0 + GLA_HEADS * GLA_DV // vw
    kernel = functools.partial(_gla_kernel, nchunks=ts // GLA_CHUNK)
    return pl.pallas_call(
        kernel,
        out_shape=jax.ShapeDtypeStruct((t, GLA_HEADS * GLA_DV), BF16),
        grid=(batch, pairs, nt),
        in_specs=[
            pl.BlockSpec((ts, kw), lambda b, p, i: (b * nt + i, q0 + p)),
            pl.BlockSpec((ts, kw), lambda b, p, i: (b * nt + i, k0 + p)),
            pl.BlockSpec((ts, vw), lambda b, p, i: (b * nt + i, v0 + p)),
            pl.BlockSpec((ts, vw), lambda b, p, i: (b * nt + i, g0 + p)),
            pl.BlockSpec((ts, kw), lambda b, p, i: (b * nt + i, p)),
            pl.BlockSpec((1, GLA_DV), lambda b, p, i: (0, 0)),
        ],
        out_specs=pl.BlockSpec((ts, vw), lambda b, p, i: (b * nt + i, p)),
        scratch_shapes=[pltpu.VMEM((vw, kw), F32)],
        compiler_params=_params("parallel", "parallel", "arbitrary"),
        name="gla",
    )(proj, proj, proj, proj, log_a, gnorm)


def _ab_outproj_kernel(h_ref, a_ref, b_ref, w_ref, o_ref):
    ka = a_ref.shape[1]
    acc = _dot(a_ref[...], w_ref[0:ka, :]) + _dot(b_ref[...], w_ref[ka:, :])
    o_ref[...] = h_ref[...] + acc


def _ab_outproj(h, a, b, w, *, tm):
    t, d = h.shape
    return pl.pallas_call(
        _ab_outproj_kernel,
        out_shape=jax.ShapeDtypeStruct((t, d), F32),
        grid=(t // tm,),
        in_specs=[
            pl.BlockSpec((tm, d), lambda i: (i, 0)),
            pl.BlockSpec((tm, a.shape[1]), lambda i: (i, 0)),
            pl.BlockSpec((tm, b.shape[1]), lambda i: (i, 0)),
            pl.BlockSpec(w.shape, lambda i: (0, 0)),
        ],
        out_specs=pl.BlockSpec((tm, d), lambda i: (i, 0)),
        compiler_params=_params("parallel"),
        name="ab_outproj",
    )(h, a, b, w)


def _mlp_kernel(h_ref, g_ref, wu_ref, wd_ref, o_ref, xn_ref):
    @pl.when(pl.program_id(1) == 0)
    def _():
        h = h_ref[...]
        xn_ref[...] = _rms(h, g_ref[...]).astype(BF16)
        o_ref[...] = h

    u = jnp.maximum(_dot(xn_ref[...], wu_ref[...]), 0.0)
    o_ref[...] += _dot((u * u).astype(BF16), wd_ref[...])


def _mlp(h, g, wu, wd, layer, *, tm, tf):
    t, d = h.shape
    f = wu.shape[2]
    return pl.pallas_call(
        _mlp_kernel,
        out_shape=jax.ShapeDtypeStruct((t, d), F32),
        grid=(t // tm, f // tf),
        in_specs=[
            pl.BlockSpec((tm, d), lambda i, j: (i, 0)),
            pl.BlockSpec((None, 1, d), lambda i, j: (layer, 0, 0)),
            pl.BlockSpec((None, d, tf), lambda i, j: (layer, 0, j)),
            pl.BlockSpec((None, tf, d), lambda i, j: (layer, j, 0)),
        ],
        out_specs=pl.BlockSpec((tm, d), lambda i, j: (i, 0)),
        scratch_shapes=[pltpu.VMEM((tm, d), BF16)],
        compiler_params=_params("parallel", "arbitrary"),
        name="mlp",
    )(h, g, wu, wd)


def _ple_kernel(h_ref, p_ref, wg_ref, wp_ref, fn_ref, o_ref, *, final_norm):
    h = h_ref[...]
    gate = _dot(h.astype(BF16), wg_ref[...])
    emb = _dot(p_ref[...].astype(BF16), wp_ref[...])
    out = h + emb * (1.0 / (1.0 + jnp.exp(-gate)))
    if final_norm:
        out = _rms(out, fn_ref[...])
    o_ref[...] = out


def _ple(h, p, wg, wp, fn, layer, *, tm, final_norm):
    t, d = h.shape
    e = p.shape[2]
    kernel = functools.partial(_ple_kernel, final_norm=final_norm)
    return pl.pallas_call(
        kernel,
        out_shape=jax.ShapeDtypeStruct((t, d), F32),
        grid=(t // tm,),
        in_specs=[
            pl.BlockSpec((tm, d), lambda i: (i, 0)),
            pl.BlockSpec((None, tm, e), lambda i: (layer, i, 0)),
            pl.BlockSpec((None, d, d), lambda i: (layer, 0, 0)),
            pl.BlockSpec((None, e, d), lambda i: (layer, 0, 0)),
            pl.BlockSpec((1, d), lambda i: (0, 0)),
        ],
        out_specs=pl.BlockSpec((tm, d), lambda i: (i, 0)),
        compiler_params=_params("parallel"),
        name="ple",
    )(h, p, wg, wp, fn)


def _ssd_kernel(xbc_ref, par_ref, dt_ref, sel_ref, y_ref, st_ref, pad_ref, *, nchunks, heads):
    c = SSD_CHUNK
    ts = xbc_ref.shape[0]
    gw = heads * SSD_HEAD_DIM
    n = SSD_STATE
    first = pl.program_id(2) == 0

    @pl.when(first)
    def _():
        st_ref[...] = jnp.zeros_like(st_ref)
        pad_ref[0:CONV_HALO, :] = jnp.zeros((CONV_HALO, pad_ref.shape[1]), F32)

    pad_ref[CONV_HALO:CONV_HALO + ts, :] = xbc_ref[...].astype(F32)
    conv = par_ref[0, 0:1, :] + par_ref[0, SSD_CONV:SSD_CONV + 1, :] * pad_ref[CONV_HALO:CONV_HALO + ts, :]
    for back in range(1, SSD_CONV):
        tap = SSD_CONV - back
        conv = conv + par_ref[0, tap:tap + 1, :] * pad_ref[CONV_HALO - back:CONV_HALO - back + ts, :]
    pad_ref[0:CONV_HALO, :] = pad_ref[ts:ts + CONV_HALO, :]
    xbc = _silu(conv)
    xs_all = xbc[:, :gw]
    bm_all = xbc[:, gw:gw + n].astype(BF16)
    cm_all = xbc[:, gw + n:].astype(BF16)

    row = lax.broadcasted_iota(jnp.int32, (c, c), 0)
    col = lax.broadcasted_iota(jnp.int32, (c, c), 1)
    tril = col <= row
    incl = jnp.where(tril, 1.0, 0.0).astype(BF16)
    lane = lax.broadcasted_iota(jnp.int32, (1, LANES), 1)
    lo_half = lane < SSD_HEAD_DIM
    sel = sel_ref[0]
    dsk = par_ref[0, 5:6, 0:gw]
    dt_bias = par_ref[0, 6:7, 0:LANES]
    a2 = -jnp.exp(par_ref[0, 7:8, 0:LANES]) * LOG2E

    for ci in range(nchunks):
        sl = slice(ci * c, (ci + 1) * c)
        xs = xs_all[sl, :]
        bm = bm_all[sl, :]
        cm = cm_all[sl, :]
        z = _dot_split_lhs(dt_ref[sl, :], sel, 2) + dt_bias
        dt = jnp.maximum(z, 0.0) + _softplus_neg_abs(z)
        a_cs = _dot_exact_rhs(incl, dt * a2)
        a_cs_t = a_cs.T
        a_last = a_cs[c - 1:c, :]
        cb = _dot_nt(cm, bm)
        y_pairs, ea_pairs, xds_pairs = [], [], []
        for pr in range(heads // 2):
            cols, ms = [], []
            for j in (2 * pr, 2 * pr + 1):
                a_col = jnp.broadcast_to(a_cs[:, j:j + 1], (c, LANES))
                decay = jnp.where(tril, jnp.exp2(a_col - a_cs_t[j:j + 1, :]), 0.0)
                ms.append((cb * decay).astype(BF16))
                cols.append((a_col, jnp.broadcast_to(dt[:, j:j + 1], (c, LANES)), a_last[:, j:j + 1]))
            pick = lambda f: jnp.where(lo_half, f(cols[0]), f(cols[1]))
            xp = xs[:, pr * LANES:(pr + 1) * LANES] * pick(lambda v: v[1])
            ea_pairs.append(pick(lambda v: jnp.exp2(v[0])))
            xds_pairs.append(xp * pick(lambda v: jnp.exp2(v[2] - v[0])))
            lhs = jnp.concatenate(ms, axis=1)
            rhs = jnp.concatenate([jnp.where(lo_half, xp, 0.0), jnp.where(lo_half, 0.0, xp)],
                                  axis=0).astype(BF16)
            y_pairs.append(_dot(lhs, rhs))
        y_diag = jnp.concatenate(y_pairs, axis=1)
        ea_x = jnp.concatenate(ea_pairs, axis=1)
        st = st_ref[...]
        y_off = _dot(cm, st.astype(BF16)) * ea_x
        st_ref[...] = st * ea_x[c - 1:c, :] + _dot_tn(bm, jnp.concatenate(xds_pairs, axis=1).astype(BF16))
        y_ref[sl, :] = (y_diag + y_off + dsk * xs).astype(BF16)


def _ssd(proj, dt_raw, params, *, batch, seq, ts, inner):
    t = proj.shape[0]
    nt = seq // ts
    g = SSD_GROUPS
    gw = inner // g
    heads = gw // SSD_HEAD_DIM
    n = SSD_STATE
    cw = gw + 2 * n

    sel = np.zeros((g, LANES, LANES), np.float32)
    for j in range(heads):
        for gi in range(g):
            sel[gi, gi * heads + j, j] = 1.0
    sel = jnp.asarray(sel, BF16)

    kernel = functools.partial(_ssd_kernel, nchunks=ts // SSD_CHUNK, heads=heads)
    row = lambda b, gi, i: b * nt + i
    return pl.pallas_call(
        kernel,
        out_shape=jax.ShapeDtypeStruct((t, inner), BF16),
        grid=(batch, g, nt),
        in_specs=[
            pl.BlockSpec((ts, cw), lambda b, gi, i: (row(b, gi, i), gi)),
            pl.BlockSpec((1, 8, cw), lambda b, gi, i: (gi, 0, 0)),
            pl.BlockSpec((ts, LANES), lambda b, gi, i: (row(b, gi, i), 0)),
            pl.BlockSpec((1, LANES, LANES), lambda b, gi, i: (gi, 0, 0)),
        ],
        out_specs=pl.BlockSpec((ts, gw), lambda b, gi, i: (row(b, gi, i), gi)),
        scratch_shapes=[
            pltpu.VMEM((n, gw), F32),
            pltpu.VMEM((ts + CONV_HALO, cw), F32),
        ],
        compiler_params=_params("parallel", "parallel", "arbitrary"),
        name="ssd",
    )(proj, params, dt_raw, sel)


def _ssd_outproj_kernel(h_ref, y_ref, zlo_ref, zhi_ref, g_ref, w_ref, o_ref, *, kc):
    inner = y_ref.shape[1]
    half = zlo_ref.shape[1]
    ssq = None
    acc = None
    for k0 in range(0, inner, kc):
        z_ref, zoff = (zlo_ref, k0) if k0 < half else (zhi_ref, k0 - half)
        z = z_ref[:, zoff:zoff + kc].astype(F32)
        yg = y_ref[:, k0:k0 + kc].astype(F32) * _silu(z)
        part = jnp.sum(yg * yg, axis=-1, keepdims=True)
        ssq = part if ssq is None else ssq + part
        term = _dot((yg * g_ref[:, k0:k0 + kc]).astype(BF16), w_ref[k0:k0 + kc, :])
        acc = term if acc is None else acc + term
    o_ref[...] = h_ref[...] + acc * lax.rsqrt(ssq * (1.0 / inner) + EPS)


def _ssd_outproj(h, y, proj, g, w, *, tm, kc):
    t, d = h.shape
    inner = y.shape[1]
    half = inner // 2
    z0 = (proj.shape[1] - inner) // half
    kernel = functools.partial(_ssd_outproj_kernel, kc=kc)
    return pl.pallas_call(
        kernel,
        out_shape=jax.ShapeDtypeStruct((t, d), F32),
        grid=(t // tm,),
        in_specs=[
            pl.BlockSpec((tm, d), lambda i: (i, 0)),
            pl.BlockSpec((tm, inner), lambda i: (i, 0)),
            pl.BlockSpec((tm, half), lambda i: (i, z0)),
            pl.BlockSpec((tm, half), lambda i: (i, z0 + 1)),
            pl.BlockSpec((1, inner), lambda i: (0, 0)),
            pl.BlockSpec((inner, d), lambda i: (0, 0), pipeline_mode=pl.Buffered(1)),
        ],
        out_specs=pl.BlockSpec((tm, d), lambda i: (i, 0)),
        compiler_params=_params("parallel"),
        name="ssd_outproj",
    )(h, y, proj, proj, g, w)


def _pad_cols(w, n):
    return jnp.pad(w, ((0, 0), (0, n - w.shape[1])))


def _group_cols(a, inner):
    g, n = SSD_GROUPS, SSD_STATE
    gw = inner // g
    b0, c0 = inner, inner + g * n
    out = []
    for gi in range(g):
        out += [a[:, gi * gw:(gi + 1) * gw], a[:, b0 + gi * n:b0 + (gi + 1) * n],
                a[:, c0 + gi * n:c0 + (gi + 1) * n]]
    return out


def _ssd_params(conv_w, conv_b, d_skip, dt_bias, a_log, inner):
    g = SSD_GROUPS
    cw = inner // g + 2 * SSD_STATE
    grouped = jnp.concatenate(_group_cols(jnp.concatenate([conv_b[None, :], conv_w], axis=0), inner),
                              axis=1)
    grouped = grouped.reshape(1 + SSD_CONV, g, cw).transpose(1, 0, 2)
    per_head = lambda v, rep: _pad_cols(jnp.repeat(v, rep).reshape(g, -1), cw)[:, None, :]
    return jnp.concatenate([grouped, per_head(d_skip, SSD_HEAD_DIM), per_head(dt_bias, 1),
                            per_head(a_log, 1)], axis=1).astype(F32)


def kernel(x, p, norm_mix, norm_mlp, ab_w_in, ab_w_gate_up, ab_b_gate, ab_gla_norm, ab_w_out,
           ssd_w_in, ssd_conv_w, ssd_conv_b, ssd_dt_bias, ssd_a_log, ssd_d, ssd_norm, ssd_w_out,
           mlp_w_up, mlp_w_down, ple_w_proj, ple_w_gate, final_norm):
    batch, seq, d = x.shape
    t = batch * seq
    depth = p.shape[0]
    tm = 512
    h = x.reshape(t, d)
    row = lambda v: v.reshape(1, -1).astype(F32)
    mlp_norm = norm_mlp.reshape(depth, 1, d).astype(F32)
    mlp_up, mlp_down = mlp_w_up.astype(BF16), mlp_w_down.astype(BF16)
    ple_gate, ple_proj = ple_w_gate.astype(BF16), ple_w_proj.astype(BF16)
    p_rows = p.reshape(depth, t, -1)

    for layer in range(depth):
        j = layer // 2
        if layer % 2 == 0:
            w = ab_w_in[j]
            glr0 = 3 * SB_HEADS * SB_DIM + 2 * GLA_HEADS * GLA_DK + GLA_HEADS * GLA_DV
            w_main = jnp.concatenate([w[:, :glr0], w[:, glr0 + GLA_RANK:]], axis=1).astype(BF16)
            w_glr = _pad_cols(w[:, glr0:glr0 + GLA_RANK], LANES).astype(BF16)
            w_gu = jnp.pad(ab_w_gate_up[j], ((0, LANES - GLA_RANK), (0, 0))).astype(BF16)
            proj, log_a = _ab_inproj(h, row(norm_mix[layer]), w_main, w_glr, w_gu,
                                     row(ab_b_gate[j]), tm=1024, tn=1024)
            o_sb = _sb_attention(proj, batch=batch, seq=seq, bq=512, bk=256)
            o_gla = _gla(proj, log_a, row(ab_gla_norm[j]), batch=batch, seq=seq, ts=512)
            h = _ab_outproj(h, o_sb, o_gla, ab_w_out[j].astype(BF16), tm=tm)
        else:
            w = ssd_w_in[j]
            inner = ssd_w_out.shape[1]
            n_zx = w.shape[1] - inner // SSD_HEAD_DIM
            w_perm = jnp.concatenate(_group_cols(w[:, inner:n_zx], inner) + [w[:, :inner]],
                                     axis=1).astype(BF16)
            proj, dt_raw = _ssd_inproj(h, row(norm_mix[layer]), w_perm,
                                       _pad_cols(w[:, n_zx:], LANES).astype(BF16), tm=1024, tn=1024)
            params = _ssd_params(ssd_conv_w[j], ssd_conv_b[j], ssd_d[j], ssd_dt_bias[j],
                                 ssd_a_log[j], inner)
            y = _ssd(proj, dt_raw, params, batch=batch, seq=seq, ts=512, inner=inner)
            h = _ssd_outproj(h, y, proj, row(ssd_norm[j]), ssd_w_out[j].astype(BF16),
                             tm=256, kc=1024)
        h = _mlp(h, mlp_norm, mlp_up, mlp_down, layer, tm=1024, tf=512)
        h = _ple(h, p_rows, ple_gate, ple_proj, row(final_norm), layer, tm=256,
                 final_norm=layer == depth - 1)
    return h.reshape(batch, seq, d)
```

```python
import functools

import numpy as np
import jax
import jax.numpy as jnp
from jax import lax
from jax.experimental import pallas as pl
from jax.experimental.pallas import tpu as pltpu

F32 = jnp.float32
BF16 = jnp.bfloat16

EPS = 1e-6
LOG2E = 1.4426950408889634
LANES = 128
VMEM_LIMIT = 56 * 1024 * 1024

SB_HEADS = 8
SB_DIM = 128
GLA_HEADS = 8
GLA_DK = 64
GLA_DV = 128
GLA_RANK = 16
GLA_TAU = 16.0
GLA_CHUNK = 64
SSD_HEAD_DIM = 64
SSD_GROUPS = 8
SSD_STATE = 128
SSD_CONV = 4
SSD_CHUNK = 128
CONV_HALO = 8


def _params(*sem):
    return pltpu.CompilerParams(dimension_semantics=sem, vmem_limit_bytes=VMEM_LIMIT)


def _dot(a, b):
    return jnp.dot(a, b, preferred_element_type=F32)


def _dot_nt(a, b):
    return lax.dot_general(a, b, (((1,), (1,)), ((), ())), preferred_element_type=F32)


def _dot_tn(a, b):
    return lax.dot_general(a, b, (((0,), (0,)), ((), ())), preferred_element_type=F32)


def _rms(x, g):
    ms = jnp.mean(x * x, axis=-1, keepdims=True)
    return x * lax.rsqrt(ms + EPS) * g


def _softplus_neg_abs(z):
    return jnp.log(1.0 + jnp.exp(-jnp.abs(z)))


def _log_sigmoid(z):
    return jnp.minimum(z, 0.0) - _softplus_neg_abs(z)


def _silu(z):
    return z * (1.0 / (1.0 + jnp.exp(-z)))


def _split3(x):
    hi = x.astype(BF16)
    r = x - hi.astype(F32)
    mid = r.astype(BF16)
    lo = (r - mid.astype(F32)).astype(BF16)
    return hi, mid, lo


def _dot_exact_rhs(m01, x):
    hi, mid, lo = _split3(x)
    return _dot(m01, hi) + _dot(m01, mid) + _dot(m01, lo)


def _dot_split_lhs(x, m01, pieces):
    out = None
    for _ in range(pieces):
        part = x.astype(BF16)
        x = x - part.astype(F32)
        term = _dot(part, m01)
        out = term if out is None else out + term
    return out


def _ab_inproj_kernel(x_ref, g_ref, w_ref, wglr_ref, wgu_ref, bg_ref, proj_ref, loga_ref, xn_ref):
    @pl.when(pl.program_id(1) == 0)
    def _():
        xb = _rms(x_ref[...], g_ref[...]).astype(BF16)
        xn_ref[...] = xb
        glr = _dot(xb, wglr_ref[...])
        ga = _dot(glr.astype(BF16), wgu_ref[...]) + bg_ref[...]
        loga_ref[...] = _log_sigmoid(ga) * (1.0 / GLA_TAU)

    proj_ref[...] = _dot(xn_ref[...], w_ref[...]).astype(BF16)


def _ab_inproj(x, g, w, wglr, wgu, bg, *, tm, tn):
    t, d = x.shape
    n = w.shape[1]
    gk = wgu.shape[1]
    return pl.pallas_call(
        _ab_inproj_kernel,
        out_shape=(jax.ShapeDtypeStruct((t, n), BF16), jax.ShapeDtypeStruct((t, gk), F32)),
        grid=(t // tm, n // tn),
        in_specs=[
            pl.BlockSpec((tm, d), lambda i, j: (i, 0)),
            pl.BlockSpec((1, d), lambda i, j: (0, 0)),
            pl.BlockSpec((d, tn), lambda i, j: (0, j)),
            pl.BlockSpec(wglr.shape, lambda i, j: (0, 0)),
            pl.BlockSpec(wgu.shape, lambda i, j: (0, 0)),
            pl.BlockSpec((1, gk), lambda i, j: (0, 0)),
        ],
        out_specs=(pl.BlockSpec((tm, tn), lambda i, j: (i, j)),
                   pl.BlockSpec((tm, gk), lambda i, j: (i, 0))),
        scratch_shapes=[pltpu.VMEM((tm, d), BF16)],
        compiler_params=_params("parallel", "arbitrary"),
        name="ab_inproj",
    )(x, g, w, wglr, wgu, bg)


def _ssd_inproj_kernel(x_ref, g_ref, w_ref, wdt_ref, proj_ref, dt_ref, xn_ref):
    @pl.when(pl.program_id(1) == 0)
    def _():
        xb = _rms(x_ref[...], g_ref[...]).astype(BF16)
        xn_ref[...] = xb
        dt_ref[...] = _dot(xb, wdt_ref[...])

    proj_ref[...] = _dot(xn_ref[...], w_ref[...]).astype(BF16)


def _ssd_inproj(x, g, w, wdt, *, tm, tn):
    t, d = x.shape
    n = w.shape[1]
    nd = wdt.shape[1]
    return pl.pallas_call(
        _ssd_inproj_kernel,
        out_shape=(jax.ShapeDtypeStruct((t, n), BF16), jax.ShapeDtypeStruct((t, nd), F32)),
        grid=(t // tm, n // tn),
        in_specs=[
            pl.BlockSpec((tm, d), lambda i, j: (i, 0)),
            pl.BlockSpec((1, d), lambda i, j: (0, 0)),
            pl.BlockSpec((d, tn), lambda i, j: (0, j)),
            pl.BlockSpec((d, nd), lambda i, j: (0, 0)),
        ],
        out_specs=(pl.BlockSpec((tm, tn), lambda i, j: (i, j)),
                   pl.BlockSpec((tm, nd), lambda i, j: (i, 0))),
        scratch_shapes=[pltpu.VMEM((tm, d), BF16)],
        compiler_params=_params("parallel", "arbitrary"),
        name="ssd_inproj",
    )(x, g, w, wdt)


def _sb_kernel(q_ref, k_ref, v_ref, o_ref, z0_ref, z1_ref, w0_ref, w1_ref, acc_ref, carry_ref,
               *, bq, bk, scale):
    i = pl.program_id(2)
    nd = bq // bk
    q = q_ref[...]
    key = lax.broadcasted_iota(jnp.int32, (bk, bq), 0)
    qry = lax.broadcasted_iota(jnp.int32, (bk, bq), 1)
    col = lax.broadcasted_iota(jnp.int32, (bk, 2 * bk), 1)
    row = lax.broadcasted_iota(jnp.int32, (bk, 2 * bk), 0)
    later2 = jnp.where(jnp.where(col >= bk, col - bk, col) > row, 1.0, 0.0).astype(BF16)

    def scores(start):
        return _dot_nt(k_ref[pl.ds(pl.multiple_of(start, bq), bq), :], q)

    def weighted_values(start, w):
        return _dot_tn(v_ref[pl.ds(pl.multiple_of(start, bq), bq), :], w)

    def block(z, carry, diagonal):
        ws = [None] * nd
        for r in reversed(range(nd)):
            z2 = z[r * bk:(r + 1) * bk, :] * (scale * LOG2E)
            pos = jnp.minimum(z2, 0.0)
            neg = pos - z2
            sp = jnp.log2(1.0 + jnp.exp2(neg + pos))
            log_beta = pos - sp
            l1 = neg - sp
            if diagonal:
                causal = key + r * bk < qry
                l1 = jnp.where(causal, l1, 0.0)
            hi = l1.astype(BF16)
            lo = (l1 - hi.astype(F32)).astype(BF16)
            within = _dot(later2, jnp.concatenate([hi, lo], axis=0))
            w = jnp.exp2(log_beta + (within + carry))
            if diagonal:
                w = jnp.where(causal, w, 0.0)
            ws[r] = w.astype(BF16)
            carry = carry + (within[0:1, :] + l1[0:1, :])
        return carry, jnp.concatenate(ws, axis=0)

    def step(m, z_cur, z_next, w_prev, w_cur):
        start = (i - 1 - m) * bq
        z_next[...] = scores(jnp.maximum(start - bq, 0))
        acc_ref[...] += weighted_values(start + bq, w_prev[...])
        carry, w = block(z_cur[...], carry_ref[...], False)
        w_cur[...] = w
        carry_ref[...] = carry

    carry, w = block(scores(i * bq), jnp.zeros((1, bq), F32), True)
    carry_ref[...] = carry
    w1_ref[...] = w
    acc_ref[...] = jnp.zeros_like(acc_ref)
    z0_ref[...] = scores(jnp.maximum(i - 1, 0) * bq)

    def two_steps(mm, _):
        step(2 * mm, z0_ref, z1_ref, w1_ref, w0_ref)
        step(2 * mm + 1, z1_ref, z0_ref, w0_ref, w1_ref)
        return 0

    lax.fori_loop(0, i // 2, two_steps, 0)
    odd = i % 2 == 1

    @pl.when(odd)
    def _():
        step(i - 1, z0_ref, z1_ref, w1_ref, w0_ref)
        acc_ref[...] += weighted_values(0, w0_ref[...])

    @pl.when(jnp.logical_not(odd))
    def _():
        acc_ref[...] += weighted_values(0, w1_ref[...])

    o_ref[...] = acc_ref[...].T.astype(BF16)


def _sb_attention(proj, *, batch, seq, bq, bk):
    t = proj.shape[0]
    nq = seq // bq
    kernel = functools.partial(_sb_kernel, bq=bq, bk=bk, scale=SB_DIM ** -0.5)
    return pl.pallas_call(
        kernel,
        out_shape=jax.ShapeDtypeStruct((t, SB_HEADS * SB_DIM), BF16),
        grid=(batch, SB_HEADS, nq),
        in_specs=[
            pl.BlockSpec((bq, SB_DIM), lambda b, h, i: (b * nq + i, h)),
            pl.BlockSpec((seq, SB_DIM), lambda b, h, i: (b, SB_HEADS + h)),
            pl.BlockSpec((seq, SB_DIM), lambda b, h, i: (b, 2 * SB_HEADS + h)),
        ],
        out_specs=pl.BlockSpec((bq, SB_DIM), lambda b, h, i: (b * nq + i, h)),
        scratch_shapes=[
            pltpu.VMEM((bq, bq), F32), pltpu.VMEM((bq, bq), F32),
            pltpu.VMEM((bq, bq), BF16), pltpu.VMEM((bq, bq), BF16),
            pltpu.VMEM((SB_DIM, bq), F32),
            pltpu.VMEM((1, bq), F32),
        ],
        compiler_params=_params("parallel", "parallel", "arbitrary"),
        name="sb_attention",
    )(proj, proj, proj)


def _gla_kernel(q_ref, k_ref, v_ref, go_ref, la_ref, gn_ref, o_ref, st_ref, *, nchunks):
    c = GLA_CHUNK

    @pl.when(pl.program_id(2) == 0)
    def _():
        st_ref[...] = jnp.zeros_like(st_ref)

    row = lax.broadcasted_iota(jnp.int32, (c, c), 0)
    col = lax.broadcasted_iota(jnp.int32, (c, c), 1)
    incl = jnp.where(col <= row, 1.0, 0.0).astype(BF16)
    causal = col <= row
    lane = lax.broadcasted_iota(jnp.int32, (1, 2 * GLA_DK), 1)
    head_lanes = (lane < GLA_DK, lane >= GLA_DK)
    srow = lax.broadcasted_iota(jnp.int32, (2 * GLA_DV, 2 * GLA_DK), 0)
    scol = lax.broadcasted_iota(jnp.int32, (2 * GLA_DV, 2 * GLA_DK), 1)
    same_head = (srow < GLA_DV) == (scol < GLA_DK)
    gn = gn_ref[...]

    for n in range(nchunks):
        sl = pl.ds(n * c, c)
        gcum = _dot_exact_rhs(incl, la_ref[sl, :])
        glast = gcum[c - 1:c, :]
        q = q_ref[sl, :].astype(F32) * (GLA_DK ** -0.5)
        k = k_ref[sl, :].astype(F32)
        v = v_ref[sl, :]
        q_dec = (q * jnp.exp(gcum)).astype(BF16)
        k_inv = (k * jnp.exp(-gcum)).astype(BF16)
        k_end = (k * jnp.exp(glast - gcum)).astype(BF16)
        st = st_ref[...]
        o_inter = _dot_nt(q_dec, st.astype(BF16))
        o_intra = []
        for a in range(2):
            qa = jnp.where(head_lanes[a], q_dec, jnp.zeros_like(q_dec))
            scores = jnp.where(causal, _dot_nt(qa, k_inv), 0.0)
            o_intra.append(_dot(scores.astype(BF16), v[:, a * GLA_DV:(a + 1) * GLA_DV]))
        st_ref[...] = st * jnp.exp(glast) + jnp.where(same_head, _dot_tn(v, k_end), 0.0)
        gate = _silu(go_ref[sl, :].astype(F32))
        outs = []
        for a in range(2):
            hs = slice(a * GLA_DV, (a + 1) * GLA_DV)
            o = o_intra[a] + o_inter[:, hs]
            outs.append(_rms(o, gn) * gate[:, hs])
        o_ref[sl, :] = jnp.concatenate(outs, axis=1).astype(BF16)


def _gla(proj, log_a, gnorm, *, batch, seq, ts):
    t = proj.shape[0]
    nt = seq // ts
    pairs = GLA_HEADS // 2
    kw, vw = 2 * GLA_DK, 2 * GLA_DV
    q0 = 3 * SB_HEADS * SB_DIM // kw
    k0 = q0 + GLA_HEADS * GLA_DK // kw
    v0 = (3 * SB_HEADS * SB_DIM + 2 * GLA_HEADS * GLA_DK) // vw
    g0 = v0 + GLA_HEADS * GLA_DV // vw
    kernel = functools.partial(_gla_kernel, nchunks=ts // GLA_CHUNK)
    return pl.pallas_call(
        kernel,
        out_shape=jax.ShapeDtypeStruct((t, GLA_HEADS * GLA_DV), BF16),
        grid=(batch, pairs, nt),
        in_specs=[
            pl.BlockSpec((ts, kw), lambda b, p, i: (b * nt + i, q0 + p)),
            pl.BlockSpec((ts, kw), lambda b, p, i: (b * nt + i, k0 + p)),
            pl.BlockSpec((ts, vw), lambda b, p, i: (b * nt + i, v0 + p)),
            pl.BlockSpec((ts, vw), lambda b, p, i: (b * nt + i, g0 + p)),
            pl.BlockSpec((ts, kw), lambda b, p, i: (b * nt + i, p)),
            pl.BlockSpec((1, GLA_DV), lambda b, p, i: (0, 0)),
        ],
        out_specs=pl.BlockSpec((ts, vw), lambda b, p, i: (b * nt + i, p)),
        scratch_shapes=[pltpu.VMEM((vw, kw), F32)],
        compiler_params=_params("parallel", "parallel", "arbitrary"),
        name="gla",
    )(proj, proj, proj, proj, log_a, gnorm)


def _ab_outproj_kernel(h_ref, a_ref, b_ref, w_ref, o_ref):
    ka = a_ref.shape[1]
    acc = _dot(a_ref[...], w_ref[0:ka, :]) + _dot(b_ref[...], w_ref[ka:, :])
    o_ref[...] = h_ref[...] + acc


def _ab_outproj(h, a, b, w, *, tm):
    t, d = h.shape
    return pl.pallas_call(
        _ab_outproj_kernel,
        out_shape=jax.ShapeDtypeStruct((t, d), F32),
        grid=(t // tm,),
        in_specs=[
            pl.BlockSpec((tm, d), lambda i: (i, 0)),
            pl.BlockSpec((tm, a.shape[1]), lambda i: (i, 0)),
            pl.BlockSpec((tm, b.shape[1]), lambda i: (i, 0)),
            pl.BlockSpec(w.shape, lambda i: (0, 0)),
        ],
        out_specs=pl.BlockSpec((tm, d), lambda i: (i, 0)),
        compiler_params=_params("parallel"),
        name="ab_outproj",
    )(h, a, b, w)


def _mlp_kernel(h_ref, g_ref, wu_ref, wd_ref, o_ref, xn_ref):
    @pl.when(pl.program_id(1) == 0)
    def _():
        h = h_ref[...]
        xn_ref[...] = _rms(h, g_ref[...]).astype(BF16)
        o_ref[...] = h

    u = jnp.maximum(_dot(xn_ref[...], wu_ref[...]), 0.0)
    o_ref[...] += _dot((u * u).astype(BF16), wd_ref[...])


def _mlp(h, g, wu, wd, layer, *, tm, tf):
    t, d = h.shape
    f = wu.shape[2]
    return pl.pallas_call(
        _mlp_kernel,
        out_shape=jax.ShapeDtypeStruct((t, d), F32),
        grid=(t // tm, f // tf),
        in_specs=[
            pl.BlockSpec((tm, d), lambda i, j: (i, 0)),
            pl.BlockSpec((None, 1, d), lambda i, j: (layer, 0, 0)),
            pl.BlockSpec((None, d, tf), lambda i, j: (layer, 0, j)),
            pl.BlockSpec((None, tf, d), lambda i, j: (layer, j, 0)),
        ],
        out_specs=pl.BlockSpec((tm, d), lambda i, j: (i, 0)),
        scratch_shapes=[pltpu.VMEM((tm, d), BF16)],
        compiler_params=_params("parallel", "arbitrary"),
        name="mlp",
    )(h, g, wu, wd)


def _ple_kernel(h_ref, p_ref, wg_ref, wp_ref, fn_ref, o_ref, *, final_norm):
    h = h_ref[...]
    gate = _dot(h.astype(BF16), wg_ref[...])
    emb = _dot(p_ref[...].astype(BF16), wp_ref[...])
    out = h + emb * (1.0 / (1.0 + jnp.exp(-gate)))
    if final_norm:
        out = _rms(out, fn_ref[...])
    o_ref[...] = out


def _ple(h, p, wg, wp, fn, layer, *, tm, final_norm):
    t, d = h.shape
    e = p.shape[2]
    kernel = functools.partial(_ple_kernel, final_norm=final_norm)
    return pl.pallas_call(
        kernel,
        out_shape=jax.ShapeDtypeStruct((t, d), F32),
        grid=(t // tm,),
        in_specs=[
            pl.BlockSpec((tm, d), lambda i: (i, 0)),
            pl.BlockSpec((None, tm, e), lambda i: (layer, i, 0)),
            pl.BlockSpec((None, d, d), lambda i: (layer, 0, 0)),
            pl.BlockSpec((None, e, d), lambda i: (layer, 0, 0)),
            pl.BlockSpec((1, d), lambda i: (0, 0)),
        ],
        out_specs=pl.BlockSpec((tm, d), lambda i: (i, 0)),
        compiler_params=_params("parallel"),
        name="ple",
    )(h, p, wg, wp, fn)


def _ssd_kernel(xbc_ref, par_ref, dt_ref, sel_ref, y_ref, st_ref, pad_ref, *, nchunks, heads):
    c = SSD_CHUNK
    ts = xbc_ref.shape[0]
    gw = heads * SSD_HEAD_DIM
    n = SSD_STATE
    first = pl.program_id(2) == 0

    @pl.when(first)
    def _():
        st_ref[...] = jnp.zeros_like(st_ref)
        pad_ref[0:CONV_HALO, :] = jnp.zeros((CONV_HALO, pad_ref.shape[1]), F32)

    pad_ref[CONV_HALO:CONV_HALO + ts, :] = xbc_ref[...].astype(F32)
    conv = par_ref[0, 0:1, :] + par_ref[0, SSD_CONV:SSD_CONV + 1, :] * pad_ref[CONV_HALO:CONV_HALO + ts, :]
    for back in range(1, SSD_CONV):
        tap = SSD_CONV - back
        conv = conv + par_ref[0, tap:tap + 1, :] * pad_ref[CONV_HALO - back:CONV_HALO - back + ts, :]
    pad_ref[0:CONV_HALO, :] = pad_ref[ts:ts + CONV_HALO, :]
    xbc = _silu(conv)
    xs_all = xbc[:, :gw]
    bm_all = xbc[:, gw:gw + n].astype(BF16)
    cm_all = xbc[:, gw + n:].astype(BF16)

    row = lax.broadcasted_iota(jnp.int32, (c, c), 0)
    col = lax.broadcasted_iota(jnp.int32, (c, c), 1)
    tril = col <= row
    incl = jnp.where(tril, 1.0, 0.0).astype(BF16)
    lane = lax.broadcasted_iota(jnp.int32, (1, LANES), 1)
    lo_half = lane < SSD_HEAD_DIM
    sel = sel_ref[0]
    dsk = par_ref[0, 5:6, 0:gw]
    dt_bias = par_ref[0, 6:7, 0:LANES]
    a2 = -jnp.exp(par_ref[0, 7:8, 0:LANES]) * LOG2E

    for ci in range(nchunks):
        sl = slice(ci * c, (ci + 1) * c)
        xs = xs_all[sl, :]
        bm = bm_all[sl, :]
        cm = cm_all[sl, :]
        z = _dot_split_lhs(dt_ref[sl, :], sel, 2) + dt_bias
        dt = jnp.maximum(z, 0.0) + _softplus_neg_abs(z)
        a_cs = _dot_exact_rhs(incl, dt * a2)
        a_cs_t = a_cs.T
        a_last = a_cs[c - 1:c, :]
        cb = _dot_nt(cm, bm)
        y_pairs, ea_pairs, xds_pairs = [], [], []
        for pr in range(heads // 2):
            cols, ms = [], []
            for j in (2 * pr, 2 * pr + 1):
                a_col = jnp.broadcast_to(a_cs[:, j:j + 1], (c, LANES))
                decay = jnp.where(tril, jnp.exp2(a_col - a_cs_t[j:j + 1, :]), 0.0)
                ms.append((cb * decay).astype(BF16))
                cols.append((a_col, jnp.broadcast_to(dt[:, j:j + 1], (c, LANES)), a_last[:, j:j + 1]))
            pick = lambda f: jnp.where(lo_half, f(cols[0]), f(cols[1]))
            xp = xs[:, pr * LANES:(pr + 1) * LANES] * pick(lambda v: v[1])
            ea_pairs.append(pick(lambda v: jnp.exp2(v[0])))
            xds_pairs.append(xp * pick(lambda v: jnp.exp2(v[2] - v[0])))
            lhs = jnp.concatenate(ms, axis=1)
            rhs = jnp.concatenate([jnp.where(lo_half, xp, 0.0), jnp.where(lo_half, 0.0, xp)],
                                  axis=0).astype(BF16)
            y_pairs.append(_dot(lhs, rhs))
        y_diag = jnp.concatenate(y_pairs, axis=1)
        ea_x = jnp.concatenate(ea_pairs, axis=1)
        st = st_ref[...]
        y_off = _dot(cm, st.astype(BF16)) * ea_x
        st_ref[...] = st * ea_x[c - 1:c, :] + _dot_tn(bm, jnp.concatenate(xds_pairs, axis=1).astype(BF16))
        y_ref[sl, :] = (y_diag + y_off + dsk * xs).astype(BF16)


def _ssd(proj, dt_raw, params, *, batch, seq, ts, inner):
    t = proj.shape[0]
    nt = seq // ts
    g = SSD_GROUPS
    gw = inner // g
    heads = gw // SSD_HEAD_DIM
    n = SSD_STATE
    cw = gw + 2 * n

    sel = np.zeros((g, LANES, LANES), np.float32)
    for j in range(heads):
        for gi in range(g):
            sel[gi, gi * heads + j, j] = 1.0
    sel = jnp.asarray(sel, BF16)

    kernel = functools.partial(_ssd_kernel, nchunks=ts // SSD_CHUNK, heads=heads)
    row = lambda b, gi, i: b * nt + i
    return pl.pallas_call(
        kernel,
        out_shape=jax.ShapeDtypeStruct((t, inner), BF16),
        grid=(batch, g, nt),
        in_specs=[
            pl.BlockSpec((ts, cw), lambda b, gi, i: (row(b, gi, i), gi)),
            pl.BlockSpec((1, 8, cw), lambda b, gi, i: (gi, 0, 0)),
            pl.BlockSpec((ts, LANES), lambda b, gi, i: (row(b, gi, i), 0)),
            pl.BlockSpec((1, LANES, LANES), lambda b, gi, i: (gi, 0, 0)),
        ],
        out_specs=pl.BlockSpec((ts, gw), lambda b, gi, i: (row(b, gi, i), gi)),
        scratch_shapes=[
            pltpu.VMEM((n, gw), F32),
            pltpu.VMEM((ts + CONV_HALO, cw), F32),
        ],
        compiler_params=_params("parallel", "parallel", "arbitrary"),
        name="ssd",
    )(proj, params, dt_raw, sel)


def _ssd_outproj_kernel(h_ref, y_ref, zlo_ref, zhi_ref, g_ref, w_ref, o_ref, *, kc):
    inner = y_ref.shape[1]
    half = zlo_ref.shape[1]
    ssq = None
    acc = None
    for k0 in range(0, inner, kc):
        z_ref, zoff = (zlo_ref, k0) if k0 < half else (zhi_ref, k0 - half)
        z = z_ref[:, zoff:zoff + kc].astype(F32)
        yg = y_ref[:, k0:k0 + kc].astype(F32) * _silu(z)
        part = jnp.sum(yg * yg, axis=-1, keepdims=True)
        ssq = part if ssq is None else ssq + part
        term = _dot((yg * g_ref[:, k0:k0 + kc]).astype(BF16), w_ref[k0:k0 + kc, :])
        acc = term if acc is None else acc + term
    o_ref[...] = h_ref[...] + acc * lax.rsqrt(ssq * (1.0 / inner) + EPS)


def _ssd_outproj(h, y, proj, g, w, *, tm, kc):
    t, d = h.shape
    inner = y.shape[1]
    half = inner // 2
    z0 = (proj.shape[1] - inner) // half
    kernel = functools.partial(_ssd_outproj_kernel, kc=kc)
    return pl.pallas_call(
        kernel,
        out_shape=jax.ShapeDtypeStruct((t, d), F32),
        grid=(t // tm,),
        in_specs=[
            pl.BlockSpec((tm, d), lambda i: (i, 0)),
            pl.BlockSpec((tm, inner), lambda i: (i, 0)),
            pl.BlockSpec((tm, half), lambda i: (i, z0)),
            pl.BlockSpec((tm, half), lambda i: (i, z0 + 1)),
            pl.BlockSpec((1, inner), lambda i: (0, 0)),
            pl.BlockSpec((inner, d), lambda i: (0, 0), pipeline_mode=pl.Buffered(1)),
        ],
        out_specs=pl.BlockSpec((tm, d), lambda i: (i, 0)),
        compiler_params=_params("parallel"),
        name="ssd_outproj",
    )(h, y, proj, proj, g, w)


def _pad_cols(w, n):
    return jnp.pad(w, ((0, 0), (0, n - w.shape[1])))


def _group_cols(a, inner):
    g, n = SSD_GROUPS, SSD_STATE
    gw = inner // g
    b0, c0 = inner, inner + g * n
    out = []
    for gi in range(g):
        out += [a[:, gi * gw:(gi + 1) * gw], a[:, b0 + gi * n:b0 + (gi + 1) * n],
                a[:, c0 + gi * n:c0 + (gi + 1) * n]]
    return out


def _ssd_params(conv_w, conv_b, d_skip, dt_bias, a_log, inner):
    g = SSD_GROUPS
    cw = inner // g + 2 * SSD_STATE
    grouped = jnp.concatenate(_group_cols(jnp.concatenate([conv_b[None, :], conv_w], axis=0), inner),
                              axis=1)
    grouped = grouped.reshape(1 + SSD_CONV, g, cw).transpose(1, 0, 2)
    per_head = lambda v, rep: _pad_cols(jnp.repeat(v, rep).reshape(g, -1), cw)[:, None, :]
    return jnp.concatenate([grouped, per_head(d_skip, SSD_HEAD_DIM), per_head(dt_bias, 1),
                            per_head(a_log, 1)], axis=1).astype(F32)


def kernel(x, p, norm_mix, norm_mlp, ab_w_in, ab_w_gate_up, ab_b_gate, ab_gla_norm, ab_w_out,
           ssd_w_in, ssd_conv_w, ssd_conv_b, ssd_dt_bias, ssd_a_log, ssd_d, ssd_norm, ssd_w_out,
           mlp_w_up, mlp_w_down, ple_w_proj, ple_w_gate, final_norm):
    batch, seq, d = x.shape
    t = batch * seq
    depth = p.shape[0]
    tm = 512
    h = x.reshape(t, d)
    row = lambda v: v.reshape(1, -1).astype(F32)
    mlp_norm = norm_mlp.reshape(depth, 1, d).astype(F32)
    mlp_up, mlp_down = mlp_w_up.astype(BF16), mlp_w_down.astype(BF16)
    ple_gate, ple_proj = ple_w_gate.astype(BF16), ple_w_proj.astype(BF16)
    p_rows = p.reshape(depth, t, -1)

    for layer in range(depth):
        j = layer // 2
        if layer % 2 == 0:
            w = ab_w_in[j]
            glr0 = 3 * SB_HEADS * SB_DIM + 2 * GLA_HEADS * GLA_DK + GLA_HEADS * GLA_DV
            w_main = jnp.concatenate([w[:, :glr0], w[:, glr0 + GLA_RANK:]], axis=1).astype(BF16)
            w_glr = _pad_cols(w[:, glr0:glr0 + GLA_RANK], LANES).astype(BF16)
            w_gu = jnp.pad(ab_w_gate_up[j], ((0, LANES - GLA_RANK), (0, 0))).astype(BF16)
            proj, log_a = _ab_inproj(h, row(norm_mix[layer]), w_main, w_glr, w_gu,
                                     row(ab_b_gate[j]), tm=1024, tn=1024)
            o_sb = _sb_attention(proj, batch=batch, seq=seq, bq=512, bk=256)
            o_gla = _gla(proj, log_a, row(ab_gla_norm[j]), batch=batch, seq=seq, ts=512)
            h = _ab_outproj(h, o_sb, o_gla, ab_w_out[j].astype(BF16), tm=tm)
        else:
            w = ssd_w_in[j]
            inner = ssd_w_out.shape[1]
            n_zx = w.shape[1] - inner // SSD_HEAD_DIM
            w_perm = jnp.concatenate(_group_cols(w[:, inner:n_zx], inner) + [w[:, :inner]],
                                     axis=1).astype(BF16)
            proj, dt_raw = _ssd_inproj(h, row(norm_mix[layer]), w_perm,
                                       _pad_cols(w[:, n_zx:], LANES).astype(BF16), tm=1024, tn=1024)
            params = _ssd_params(ssd_conv_w[j], ssd_conv_b[j], ssd_d[j], ssd_dt_bias[j],
                                 ssd_a_log[j], inner)
            y = _ssd(proj, dt_raw, params, batch=batch, seq=seq, ts=512, inner=inner)
            h = _ssd_outproj(h, y, proj, row(ssd_norm[j]), ssd_w_out[j].astype(BF16),
                             tm=256, kc=1024)
        h = _mlp(h, mlp_norm, mlp_up, mlp_down, layer, tm=1024, tf=512)
        h = _ple(h, p_rows, ple_gate, ple_proj, row(final_norm), layer, tm=256,
                 final_norm=layer == depth - 1)
    return h.reshape(batch, seq, d)
```

```python
import functools

import numpy as np
import jax
import jax.numpy as jnp
from jax import lax
from jax.experimental import pallas as pl
from jax.experimental.pallas import tpu as pltpu

F32 = jnp.float32
BF16 = jnp.bfloat16

EPS = 1e-6
LOG2E = 1.4426950408889634
LANES = 128
VMEM_LIMIT = 56 * 1024 * 1024

SB_HEADS = 8
SB_DIM = 128
GLA_HEADS = 8
GLA_DK = 64
GLA_DV = 128
GLA_RANK = 16
GLA_TAU = 16.0
GLA_CHUNK = 64
SSD_HEAD_DIM = 64
SSD_GROUPS = 8
SSD_STATE = 128
SSD_CONV = 4
SSD_CHUNK = 128
CONV_HALO = 8


def _params(*sem):
    return pltpu.CompilerParams(dimension_semantics=sem, vmem_limit_bytes=VMEM_LIMIT)


def _dot(a, b):
    return jnp.dot(a, b, preferred_element_type=F32)


def _dot_nt(a, b):
    return lax.dot_general(a, b, (((1,), (1,)), ((), ())), preferred_element_type=F32)


def _dot_tn(a, b):
    return lax.dot_general(a, b, (((0,), (0,)), ((), ())), preferred_element_type=F32)


def _rms(x, g):
    ms = jnp.mean(x * x, axis=-1, keepdims=True)
    return x * lax.rsqrt(ms + EPS) * g


def _softplus_neg_abs(z):
    return jnp.log(1.0 + jnp.exp(-jnp.abs(z)))


def _log_sigmoid(z):
    return jnp.minimum(z, 0.0) - _softplus_neg_abs(z)


def _silu(z):
    return z * (1.0 / (1.0 + jnp.exp(-z)))


def _split3(x):
    hi = x.astype(BF16)
    r = x - hi.astype(F32)
    mid = r.astype(BF16)
    lo = (r - mid.astype(F32)).astype(BF16)
    return hi, mid, lo


def _dot_exact_rhs(m01, x):
    return _dot(jnp.concatenate([m01] * 3, axis=1), jnp.concatenate(_split3(x), axis=0))


def _dot_split_lhs(x, m01, pieces):
    parts = []
    for _ in range(pieces):
        part = x.astype(BF16)
        x = x - part.astype(F32)
        parts.append(part)
    return _dot(jnp.concatenate(parts, axis=1), jnp.concatenate([m01] * pieces, axis=0))


def _ab_inproj_kernel(x_ref, g_ref, w_ref, wglr_ref, wgu_ref, bg_ref, proj_ref, loga_ref, xn_ref):
    @pl.when(pl.program_id(1) == 0)
    def _():
        xb = _rms(x_ref[...], g_ref[...]).astype(BF16)
        xn_ref[...] = xb
        glr = _dot(xb, wglr_ref[...])
        ga = _dot(glr.astype(BF16), wgu_ref[...]) + bg_ref[...]
        loga_ref[...] = _log_sigmoid(ga) * (1.0 / GLA_TAU)

    proj_ref[...] = _dot(xn_ref[...], w_ref[...]).astype(BF16)


def _ab_inproj(x, g, w, wglr, wgu, bg, *, tm, tn):
    t, d = x.shape
    n = w.shape[1]
    gk = wgu.shape[1]
    return pl.pallas_call(
        _ab_inproj_kernel,
        out_shape=(jax.ShapeDtypeStruct((t, n), BF16), jax.ShapeDtypeStruct((t, gk), F32)),
        grid=(t // tm, n // tn),
        in_specs=[
            pl.BlockSpec((tm, d), lambda i, j: (i, 0)),
            pl.BlockSpec((1, d), lambda i, j: (0, 0)),
            pl.BlockSpec((d, tn), lambda i, j: (0, j)),
            pl.BlockSpec(wglr.shape, lambda i, j: (0, 0)),
            pl.BlockSpec(wgu.shape, lambda i, j: (0, 0)),
            pl.BlockSpec((1, gk), lambda i, j: (0, 0)),
        ],
        out_specs=(pl.BlockSpec((tm, tn), lambda i, j: (i, j)),
                   pl.BlockSpec((tm, gk), lambda i, j: (i, 0))),
        scratch_shapes=[pltpu.VMEM((tm, d), BF16)],
        compiler_params=_params("parallel", "arbitrary"),
        name="ab_inproj",
    )(x, g, w, wglr, wgu, bg)


def _ssd_inproj_kernel(x_ref, g_ref, w_ref, wdt_ref, proj_ref, dt_ref, xn_ref):
    @pl.when(pl.program_id(1) == 0)
    def _():
        xb = _rms(x_ref[...], g_ref[...]).astype(BF16)
        xn_ref[...] = xb
        dt_ref[...] = _dot(xb, wdt_ref[...])

    proj_ref[...] = _dot(xn_ref[...], w_ref[...]).astype(BF16)


def _ssd_inproj(x, g, w, wdt, *, tm, tn):
    t, d = x.shape
    n = w.shape[1]
    nd = wdt.shape[1]
    return pl.pallas_call(
        _ssd_inproj_kernel,
        out_shape=(jax.ShapeDtypeStruct((t, n), BF16), jax.ShapeDtypeStruct((t, nd), F32)),
        grid=(t // tm, n // tn),
        in_specs=[
            pl.BlockSpec((tm, d), lambda i, j: (i, 0)),
            pl.BlockSpec((1, d), lambda i, j: (0, 0)),
            pl.BlockSpec((d, tn), lambda i, j: (0, j)),
            pl.BlockSpec((d, nd), lambda i, j: (0, 0)),
        ],
        out_specs=(pl.BlockSpec((tm, tn), lambda i, j: (i, j)),
                   pl.BlockSpec((tm, nd), lambda i, j: (i, 0))),
        scratch_shapes=[pltpu.VMEM((tm, d), BF16)],
        compiler_params=_params("parallel", "arbitrary"),
        name="ssd_inproj",
    )(x, g, w, wdt)


def _sb_kernel(q_ref, k_ref, v_ref, o_ref, z0_ref, z1_ref, w0_ref, w1_ref, acc_ref, carry_ref,
               *, bq, bk, scale):
    i = pl.program_id(2)
    nd = bq // bk
    q = q_ref[...]
    key = lax.broadcasted_iota(jnp.int32, (bk, bq), 0)
    qry = lax.broadcasted_iota(jnp.int32, (bk, bq), 1)
    col = lax.broadcasted_iota(jnp.int32, (bk, 2 * bk), 1)
    row = lax.broadcasted_iota(jnp.int32, (bk, 2 * bk), 0)
    later2 = jnp.where(jnp.where(col >= bk, col - bk, col) > row, 1.0, 0.0).astype(BF16)

    def scores(start):
        return _dot_nt(k_ref[pl.ds(pl.multiple_of(start, bq), bq), :], q)

    def weighted_values(start, w):
        return _dot_tn(v_ref[pl.ds(pl.multiple_of(start, bq), bq), :], w)

    def block(z, carry, diagonal):
        ws = [None] * nd
        for r in reversed(range(nd)):
            z2 = z[r * bk:(r + 1) * bk, :] * (scale * LOG2E)
            pos = jnp.minimum(z2, 0.0)
            neg = pos - z2
            sp = jnp.log2(1.0 + jnp.exp2(neg + pos))
            log_beta = pos - sp
            l1 = neg - sp
            if diagonal:
                causal = key + r * bk < qry
                l1 = jnp.where(causal, l1, 0.0)
            hi = l1.astype(BF16)
            lo = (l1 - hi.astype(F32)).astype(BF16)
            within = _dot(later2, jnp.concatenate([hi, lo], axis=0))
            w = jnp.exp2(log_beta + (within + carry))
            if diagonal:
                w = jnp.where(causal, w, 0.0)
            ws[r] = w.astype(BF16)
            carry = carry + (within[0:1, :] + l1[0:1, :])
        return carry, jnp.concatenate(ws, axis=0)

    def step(m, z_cur, z_next, w_prev, w_cur):
        start = (i - 1 - m) * bq
        z_next[...] = scores(jnp.maximum(start - bq, 0))
        acc_ref[...] += weighted_values(start + bq, w_prev[...])
        carry, w = block(z_cur[...], carry_ref[...], False)
        w_cur[...] = w
        carry_ref[...] = carry

    carry, w = block(scores(i * bq), jnp.zeros((1, bq), F32), True)
    carry_ref[...] = carry
    w1_ref[...] = w
    acc_ref[...] = jnp.zeros_like(acc_ref)
    z0_ref[...] = scores(jnp.maximum(i - 1, 0) * bq)

    def two_steps(mm, _):
        step(2 * mm, z0_ref, z1_ref, w1_ref, w0_ref)
        step(2 * mm + 1, z1_ref, z0_ref, w0_ref, w1_ref)
        return 0

    lax.fori_loop(0, i // 2, two_steps, 0)
    odd = i % 2 == 1

    @pl.when(odd)
    def _():
        step(i - 1, z0_ref, z1_ref, w1_ref, w0_ref)
        acc_ref[...] += weighted_values(0, w0_ref[...])

    @pl.when(jnp.logical_not(odd))
    def _():
        acc_ref[...] += weighted_values(0, w1_ref[...])

    o_ref[...] = acc_ref[...].T.astype(BF16)


def _sb_attention(proj, *, batch, seq, bq, bk):
    t = proj.shape[0]
    nq = seq // bq
    kernel = functools.partial(_sb_kernel, bq=bq, bk=bk, scale=SB_DIM ** -0.5)
    return pl.pallas_call(
        kernel,
        out_shape=jax.ShapeDtypeStruct((t, SB_HEADS * SB_DIM), BF16),
        grid=(batch, SB_HEADS, nq),
        in_specs=[
            pl.BlockSpec((bq, SB_DIM), lambda b, h, i: (b * nq + i, h)),
            pl.BlockSpec((seq, SB_DIM), lambda b, h, i: (b, SB_HEADS + h)),
            pl.BlockSpec((seq, SB_DIM), lambda b, h, i: (b, 2 * SB_HEADS + h)),
        ],
        out_specs=pl.BlockSpec((bq, SB_DIM), lambda b, h, i: (b * nq + i, h)),
        scratch_shapes=[
            pltpu.VMEM((bq, bq), F32), pltpu.VMEM((bq, bq), F32),
            pltpu.VMEM((bq, bq), BF16), pltpu.VMEM((bq, bq), BF16),
            pltpu.VMEM((SB_DIM, bq), F32),
            pltpu.VMEM((1, bq), F32),
        ],
        compiler_params=_params("parallel", "parallel", "arbitrary"),
        name="sb_attention",
    )(proj, proj, proj)


def _gla_kernel(q_ref, k_ref, v_ref, go_ref, la_ref, gn_ref, o_ref, st_ref, *, nchunks):
    c = GLA_CHUNK

    @pl.when(pl.program_id(2) == 0)
    def _():
        st_ref[...] = jnp.zeros_like(st_ref)

    kw, vw = 2 * GLA_DK, 2 * GLA_DV
    row = lax.broadcasted_iota(jnp.int32, (c, 3 * c), 0)
    col = lax.broadcasted_iota(jnp.int32, (c, 3 * c), 1)
    incl3 = jnp.where(col % c <= row, 1.0, 0.0).astype(BF16)
    srow = lax.broadcasted_iota(jnp.int32, (c, kw), 0)
    scol = lax.broadcasted_iota(jnp.int32, (c, kw), 1)
    causal2 = scol % c <= srow
    krow = lax.broadcasted_iota(jnp.int32, (kw, kw), 0)
    kcol = lax.broadcasted_iota(jnp.int32, (kw, kw), 1)
    key_head = (krow < c) == (kcol < GLA_DK)
    vrow = lax.broadcasted_iota(jnp.int32, (kw, vw), 0)
    vcol = lax.broadcasted_iota(jnp.int32, (kw, vw), 1)
    val_head = (vrow < c) == (vcol < GLA_DV)
    trow = lax.broadcasted_iota(jnp.int32, (vw, kw), 0)
    tcol = lax.broadcasted_iota(jnp.int32, (vw, kw), 1)
    same_head = (trow < GLA_DV) == (tcol < GLA_DK)
    gn = gn_ref[...]
    st = st_ref[...]

    for n in range(nchunks):
        sl = pl.ds(n * c, c)
        gcum = _dot(incl3, jnp.concatenate(_split3(la_ref[sl, :]), axis=0))
        glast = gcum[c - 1:c, :]
        q = q_ref[sl, :].astype(F32) * (GLA_DK ** -0.5)
        k = k_ref[sl, :].astype(F32)
        v = v_ref[sl, :]
        q_dec = (q * jnp.exp(gcum)).astype(BF16)
        k_inv = (k * jnp.exp(-gcum)).astype(BF16)
        k_end = (k * jnp.exp(glast - gcum)).astype(BF16)
        k2 = jnp.where(key_head, jnp.concatenate([k_inv, k_inv], axis=0), jnp.zeros((kw, kw), BF16))
        scores = jnp.where(causal2, _dot_nt(q_dec, k2), 0.0)
        v2 = jnp.where(val_head, jnp.concatenate([v, v], axis=0), jnp.zeros((kw, vw), BF16))
        o = _dot(scores.astype(BF16), v2) + _dot_nt(q_dec, st.astype(BF16))
        st = st * jnp.exp(glast) + jnp.where(same_head, _dot_tn(v, k_end), 0.0)
        gate = _silu(go_ref[sl, :].astype(F32))
        outs = []
        for a in range(2):
            hs = slice(a * GLA_DV, (a + 1) * GLA_DV)
            outs.append(_rms(o[:, hs], gn) * gate[:, hs])
        o_ref[sl, :] = jnp.concatenate(outs, axis=1).astype(BF16)
    st_ref[...] = st


def _gla(proj, log_a, gnorm, *, batch, seq, ts):
    t = proj.shape[0]
    nt = seq // ts
    pairs = GLA_HEADS // 2
    kw, vw = 2 * GLA_DK, 2 * GLA_DV
    q0 = 3 * SB_HEADS * SB_DIM // kw
    k0 = q0 + GLA_HEADS * GLA_DK // kw
    v0 = (3 * SB_HEADS * SB_DIM + 2 * GLA_HEADS * GLA_DK) // vw
    g0 = v0 + GLA_HEADS * GLA_DV // vw
    kernel = functools.partial(_gla_kernel, nchunks=ts // GLA_CHUNK)
    return pl.pallas_call(
        kernel,
        out_shape=jax.ShapeDtypeStruct((t, GLA_HEADS * GLA_DV), BF16),
        grid=(batch, pairs, nt),
        in_specs=[
            pl.BlockSpec((ts, kw), lambda b, p, i: (b * nt + i, q0 + p)),
            pl.BlockSpec((ts, kw), lambda b, p, i: (b * nt + i, k0 + p)),
            pl.BlockSpec((ts, vw), lambda b, p, i: (b * nt + i, v0 + p)),
            pl.BlockSpec((ts, vw), lambda b, p, i: (b * nt + i, g0 + p)),
            pl.BlockSpec((ts, kw), lambda b, p, i: (b * nt + i, p)),
            pl.BlockSpec((1, GLA_DV), lambda b, p, i: (0, 0)),
        ],
        out_specs=pl.BlockSpec((ts, vw), lambda b, p, i: (b * nt + i, p)),
        scratch_shapes=[pltpu.VMEM((vw, kw), F32)],
        compiler_params=_params("parallel", "parallel", "arbitrary"),
        name="gla",
    )(proj, proj, proj, proj, log_a, gnorm)


def _ab_outproj_kernel(h_ref, a_ref, b_ref, w_ref, o_ref):
    ka = a_ref.shape[1]
    acc = _dot(a_ref[...], w_ref[0:ka, :]) + _dot(b_ref[...], w_ref[ka:, :])
    o_ref[...] = h_ref[...] + acc


def _ab_outproj(h, a, b, w, *, tm):
    t, d = h.shape
    return pl.pallas_call(
        _ab_outproj_kernel,
        out_shape=jax.ShapeDtypeStruct((t, d), F32),
        grid=(t // tm,),
        in_specs=[
            pl.BlockSpec((tm, d), lambda i: (i, 0)),
            pl.BlockSpec((tm, a.shape[1]), lambda i: (i, 0)),
            pl.BlockSpec((tm, b.shape[1]), lambda i: (i, 0)),
            pl.BlockSpec(w.shape, lambda i: (0, 0)),
        ],
        out_specs=pl.BlockSpec((tm, d), lambda i: (i, 0)),
        compiler_params=_params("parallel"),
        name="ab_outproj",
    )(h, a, b, w)


def _mlp_kernel(h_ref, g_ref, wu_ref, wd_ref, o_ref, xn_ref):
    @pl.when(pl.program_id(1) == 0)
    def _():
        h = h_ref[...]
        xn_ref[...] = _rms(h, g_ref[...]).astype(BF16)
        o_ref[...] = h

    u = jnp.maximum(_dot(xn_ref[...], wu_ref[...]), 0.0)
    o_ref[...] += _dot((u * u).astype(BF16), wd_ref[...])


def _mlp(h, g, wu, wd, layer, *, tm, tf):
    t, d = h.shape
    f = wu.shape[2]
    return pl.pallas_call(
        _mlp_kernel,
        out_shape=jax.ShapeDtypeStruct((t, d), F32),
        grid=(t // tm, f // tf),
        in_specs=[
            pl.BlockSpec((tm, d), lambda i, j: (i, 0)),
            pl.BlockSpec((None, 1, d), lambda i, j: (layer, 0, 0)),
            pl.BlockSpec((None, d, tf), lambda i, j: (layer, 0, j)),
            pl.BlockSpec((None, tf, d), lambda i, j: (layer, j, 0)),
        ],
        out_specs=pl.BlockSpec((tm, d), lambda i, j: (i, 0)),
        scratch_shapes=[pltpu.VMEM((tm, d), BF16)],
        compiler_params=_params("parallel", "arbitrary"),
        name="mlp",
    )(h, g, wu, wd)


def _ple_kernel(h_ref, p_ref, wg_ref, wp_ref, fn_ref, o_ref, *, final_norm):
    h = h_ref[...]
    gate = _dot(h.astype(BF16), wg_ref[...])
    emb = _dot(p_ref[...].astype(BF16), wp_ref[...])
    out = h + emb * (1.0 / (1.0 + jnp.exp(-gate)))
    if final_norm:
        out = _rms(out, fn_ref[...])
    o_ref[...] = out


def _ple(h, p, wg, wp, fn, layer, *, tm, final_norm):
    t, d = h.shape
    e = p.shape[2]
    kernel = functools.partial(_ple_kernel, final_norm=final_norm)
    return pl.pallas_call(
        kernel,
        out_shape=jax.ShapeDtypeStruct((t, d), F32),
        grid=(t // tm,),
        in_specs=[
            pl.BlockSpec((tm, d), lambda i: (i, 0)),
            pl.BlockSpec((None, tm, e), lambda i: (layer, i, 0)),
            pl.BlockSpec((None, d, d), lambda i: (layer, 0, 0)),
            pl.BlockSpec((None, e, d), lambda i: (layer, 0, 0)),
            pl.BlockSpec((1, d), lambda i: (0, 0)),
        ],
        out_specs=pl.BlockSpec((tm, d), lambda i: (i, 0)),
        compiler_params=_params("parallel"),
        name="ple",
    )(h, p, wg, wp, fn)


def _ssd_kernel(xbc_ref, par_ref, dt_ref, sel_ref, y_ref, st_ref, pad_ref, *, nchunks, heads):
    c = SSD_CHUNK
    ts = xbc_ref.shape[0]
    gw = heads * SSD_HEAD_DIM
    n = SSD_STATE
    first = pl.program_id(2) == 0

    @pl.when(first)
    def _():
        st_ref[...] = jnp.zeros_like(st_ref)
        pad_ref[0:CONV_HALO, :] = jnp.zeros((CONV_HALO, pad_ref.shape[1]), F32)

    pad_ref[CONV_HALO:CONV_HALO + ts, :] = xbc_ref[...].astype(F32)
    conv = par_ref[0, 0:1, :] + par_ref[0, SSD_CONV:SSD_CONV + 1, :] * pad_ref[CONV_HALO:CONV_HALO + ts, :]
    for back in range(1, SSD_CONV):
        tap = SSD_CONV - back
        conv = conv + par_ref[0, tap:tap + 1, :] * pad_ref[CONV_HALO - back:CONV_HALO - back + ts, :]
    pad_ref[0:CONV_HALO, :] = pad_ref[ts:ts + CONV_HALO, :]
    xbc = _silu(conv)
    xs_all = xbc[:, :gw]
    bm_all = xbc[:, gw:gw + n].astype(BF16)
    cm_all = xbc[:, gw + n:].astype(BF16)

    row = lax.broadcasted_iota(jnp.int32, (c, c), 0)
    col = lax.broadcasted_iota(jnp.int32, (c, c), 1)
    tril = col <= row
    incl = jnp.where(tril, 1.0, 0.0).astype(BF16)
    lane = lax.broadcasted_iota(jnp.int32, (1, LANES), 1)
    lo_half = lane < SSD_HEAD_DIM
    sel = sel_ref[0]
    dsk = par_ref[0, 5:6, 0:gw]
    dt_bias = par_ref[0, 6:7, 0:LANES]
    a2 = -jnp.exp(par_ref[0, 7:8, 0:LANES]) * LOG2E
    st = st_ref[...]

    for ci in range(nchunks):
        sl = slice(ci * c, (ci + 1) * c)
        xs = xs_all[sl, :]
        bm = bm_all[sl, :]
        cm = cm_all[sl, :]
        z = _dot_split_lhs(dt_ref[sl, :], sel, 2) + dt_bias
        dt = jnp.maximum(z, 0.0) + _softplus_neg_abs(z)
        a_cs = _dot_exact_rhs(incl, dt * a2)
        a_cs_t = a_cs.T
        a_last = a_cs[c - 1:c, :]
        cb = _dot_nt(cm, bm)
        y_pairs, ea_pairs, xds_pairs = [], [], []
        for pr in range(heads // 2):
            cols, ms = [], []
            for j in (2 * pr, 2 * pr + 1):
                a_col = jnp.broadcast_to(a_cs[:, j:j + 1], (c, LANES))
                decay = jnp.where(tril, jnp.exp2(a_col - a_cs_t[j:j + 1, :]), 0.0)
                ms.append((cb * decay).astype(BF16))
                cols.append((a_col, jnp.broadcast_to(dt[:, j:j + 1], (c, LANES)), a_last[:, j:j + 1]))
            pick = lambda f: jnp.where(lo_half, f(cols[0]), f(cols[1]))
            xp = xs[:, pr * LANES:(pr + 1) * LANES] * pick(lambda v: v[1])
            ea_pairs.append(pick(lambda v: jnp.exp2(v[0])))
            xds_pairs.append(xp * pick(lambda v: jnp.exp2(v[2] - v[0])))
            lhs = jnp.concatenate(ms, axis=1)
            rhs = jnp.concatenate([jnp.where(lo_half, xp, 0.0), jnp.where(lo_half, 0.0, xp)],
                                  axis=0).astype(BF16)
            y_pairs.append(_dot(lhs, rhs))
        y_diag = jnp.concatenate(y_pairs, axis=1)
        ea_x = jnp.concatenate(ea_pairs, axis=1)
        y_off = _dot(cm, st.astype(BF16)) * ea_x
        st = st * ea_x[c - 1:c, :] + _dot_tn(bm, jnp.concatenate(xds_pairs, axis=1).astype(BF16))
        y_ref[sl, :] = (y_diag + y_off + dsk * xs).astype(BF16)
    st_ref[...] = st


def _ssd(proj, dt_raw, params, *, batch, seq, ts, inner):
    t = proj.shape[0]
    nt = seq // ts
    g = SSD_GROUPS
    gw = inner // g
    heads = gw // SSD_HEAD_DIM
    n = SSD_STATE
    cw = gw + 2 * n

    sel = np.zeros((g, LANES, LANES), np.float32)
    for j in range(heads):
        for gi in range(g):
            sel[gi, gi * heads + j, j] = 1.0
    sel = jnp.asarray(sel, BF16)

    kernel = functools.partial(_ssd_kernel, nchunks=ts // SSD_CHUNK, heads=heads)
    row = lambda b, gi, i: b * nt + i
    return pl.pallas_call(
        kernel,
        out_shape=jax.ShapeDtypeStruct((t, inner), BF16),
        grid=(batch, g, nt),
        in_specs=[
            pl.BlockSpec((ts, cw), lambda b, gi, i: (row(b, gi, i), gi)),
            pl.BlockSpec((1, 8, cw), lambda b, gi, i: (gi, 0, 0)),
            pl.BlockSpec((ts, LANES), lambda b, gi, i: (row(b, gi, i), 0)),
            pl.BlockSpec((1, LANES, LANES), lambda b, gi, i: (gi, 0, 0)),
        ],
        out_specs=pl.BlockSpec((ts, gw), lambda b, gi, i: (row(b, gi, i), gi)),
        scratch_shapes=[
            pltpu.VMEM((n, gw), F32),
            pltpu.VMEM((ts + CONV_HALO, cw), F32),
        ],
        compiler_params=_params("parallel", "parallel", "arbitrary"),
        name="ssd",
    )(proj, params, dt_raw, sel)


def _ssd_outproj_kernel(h_ref, y_ref, zlo_ref, zhi_ref, g_ref, w_ref, o_ref, *, kc):
    inner = y_ref.shape[1]
    half = zlo_ref.shape[1]
    ssq = None
    acc = None
    for k0 in range(0, inner, kc):
        z_ref, zoff = (zlo_ref, k0) if k0 < half else (zhi_ref, k0 - half)
        z = z_ref[:, zoff:zoff + kc].astype(F32)
        yg = y_ref[:, k0:k0 + kc].astype(F32) * _silu(z)
        part = jnp.sum(yg * yg, axis=-1, keepdims=True)
        ssq = part if ssq is None else ssq + part
        term = _dot((yg * g_ref[:, k0:k0 + kc]).astype(BF16), w_ref[k0:k0 + kc, :])
        acc = term if acc is None else acc + term
    o_ref[...] = h_ref[...] + acc * lax.rsqrt(ssq * (1.0 / inner) + EPS)


def _ssd_outproj(h, y, proj, g, w, *, tm, kc):
    t, d = h.shape
    inner = y.shape[1]
    half = inner // 2
    z0 = (proj.shape[1] - inner) // half
    kernel = functools.partial(_ssd_outproj_kernel, kc=kc)
    return pl.pallas_call(
        kernel,
        out_shape=jax.ShapeDtypeStruct((t, d), F32),
        grid=(t // tm,),
        in_specs=[
            pl.BlockSpec((tm, d), lambda i: (i, 0)),
            pl.BlockSpec((tm, inner), lambda i: (i, 0)),
            pl.BlockSpec((tm, half), lambda i: (i, z0)),
            pl.BlockSpec((tm, half), lambda i: (i, z0 + 1)),
            pl.BlockSpec((1, inner), lambda i: (0, 0)),
            pl.BlockSpec((inner, d), lambda i: (0, 0), pipeline_mode=pl.Buffered(1)),
        ],
        out_specs=pl.BlockSpec((tm, d), lambda i: (i, 0)),
        compiler_params=_params("parallel"),
        name="ssd_outproj",
    )(h, y, proj, proj, g, w)


def _pad_cols(w, n):
    return jnp.pad(w, ((0, 0), (0, n - w.shape[1])))


def _group_cols(a, inner):
    g, n = SSD_GROUPS, SSD_STATE
    gw = inner // g
    b0, c0 = inner, inner + g * n
    out = []
    for gi in range(g):
        out += [a[:, gi * gw:(gi + 1) * gw], a[:, b0 + gi * n:b0 + (gi + 1) * n],
                a[:, c0 + gi * n:c0 + (gi + 1) * n]]
    return out


def _ssd_params(conv_w, conv_b, d_skip, dt_bias, a_log, inner):
    g = SSD_GROUPS
    cw = inner // g + 2 * SSD_STATE
    grouped = jnp.concatenate(_group_cols(jnp.concatenate([conv_b[None, :], conv_w], axis=0), inner),
                              axis=1)
    grouped = grouped.reshape(1 + SSD_CONV, g, cw).transpose(1, 0, 2)
    per_head = lambda v, rep: _pad_cols(jnp.repeat(v, rep).reshape(g, -1), cw)[:, None, :]
    return jnp.concatenate([grouped, per_head(d_skip, SSD_HEAD_DIM), per_head(dt_bias, 1),
                            per_head(a_log, 1)], axis=1).astype(F32)


def kernel(x, p, norm_mix, norm_mlp, ab_w_in, ab_w_gate_up, ab_b_gate, ab_gla_norm, ab_w_out,
           ssd_w_in, ssd_conv_w, ssd_conv_b, ssd_dt_bias, ssd_a_log, ssd_d, ssd_norm, ssd_w_out,
           mlp_w_up, mlp_w_down, ple_w_proj, ple_w_gate, final_norm):
    batch, seq, d = x.shape
    t = batch * seq
    depth = p.shape[0]
    tm = 512
    h = x.reshape(t, d)
    row = lambda v: v.reshape(1, -1).astype(F32)
    mlp_norm = norm_mlp.reshape(depth, 1, d).astype(F32)
    mlp_up, mlp_down = mlp_w_up.astype(BF16), mlp_w_down.astype(BF16)
    ple_gate, ple_proj = ple_w_gate.astype(BF16), ple_w_proj.astype(BF16)
    p_rows = p.reshape(depth, t, -1)

    for layer in range(depth):
        j = layer // 2
        if layer % 2 == 0:
            w = ab_w_in[j].astype(BF16)
            glr0 = 3 * SB_HEADS * SB_DIM + 2 * GLA_HEADS * GLA_DK + GLA_HEADS * GLA_DV
            w_main = jnp.concatenate([w[:, :glr0], w[:, glr0 + GLA_RANK:]], axis=1)
            w_glr = _pad_cols(w[:, glr0:glr0 + GLA_RANK], LANES)
            w_gu = jnp.pad(ab_w_gate_up[j], ((0, LANES - GLA_RANK), (0, 0))).astype(BF16)
            proj, log_a = _ab_inproj(h, row(norm_mix[layer]), w_main, w_glr, w_gu,
                                     row(ab_b_gate[j]), tm=1024, tn=1024)
            o_sb = _sb_attention(proj, batch=batch, seq=seq, bq=512, bk=256)
            o_gla = _gla(proj, log_a, row(ab_gla_norm[j]), batch=batch, seq=seq, ts=512)
            h = _ab_outproj(h, o_sb, o_gla, ab_w_out[j].astype(BF16), tm=tm)
        else:
            w = ssd_w_in[j].astype(BF16)
            inner = ssd_w_out.shape[1]
            n_zx = w.shape[1] - inner // SSD_HEAD_DIM
            w_perm = jnp.concatenate(_group_cols(w[:, inner:n_zx], inner) + [w[:, :inner]], axis=1)
            proj, dt_raw = _ssd_inproj(h, row(norm_mix[layer]), w_perm,
                                       _pad_cols(w[:, n_zx:], LANES), tm=1024, tn=1024)
            params = _ssd_params(ssd_conv_w[j], ssd_conv_b[j], ssd_d[j], ssd_dt_bias[j],
                                 ssd_a_log[j], inner)
            y = _ssd(proj, dt_raw, params, batch=batch, seq=seq, ts=1024, inner=inner)
            h = _ssd_outproj(h, y, proj, row(ssd_norm[j]), ssd_w_out[j].astype(BF16),
                             tm=256, kc=1024)
        h = _mlp(h, mlp_norm, mlp_up, mlp_down, layer, tm=1024, tf=512)
        h = _ple(h, p_rows, ple_gate, ple_proj, row(final_norm), layer, tm=256,
                 final_norm=layer == depth - 1)
    return h.reshape(batch, seq, d)
```

```python
import functools
from typing import NamedTuple

import numpy as np
import jax
import jax.numpy as jnp
from jax import lax
from jax.experimental import pallas as pl
from jax.experimental.pallas import tpu as pltpu

F32 = jnp.float32
BF16 = jnp.bfloat16

EPS = 1e-6
LOG2E = 1.4426950408889634
LANES = 128
VMEM_LIMIT = 56 * 1024 * 1024

SB_HEADS = 8
SB_DIM = 128
GLA_HEADS = 8
GLA_DK = 64
GLA_DV = 128
GLA_RANK = 16
GLA_TAU = 16.0
GLA_CHUNK = 64
SSD_HEAD_DIM = 64
SSD_GROUPS = 8
SSD_STATE = 128
SSD_CONV = 4
SSD_CHUNK = 128
CONV_HALO = 8


def _params(*sem):
    return pltpu.CompilerParams(dimension_semantics=sem, vmem_limit_bytes=VMEM_LIMIT)


class _Tiles(NamedTuple):
    proj_rows: int
    proj_cols: int
    out_rows: int
    mlp_rows: int
    mlp_hidden: int
    ple_rows: int
    ssd_out_rows: int
    ssd_out_k: int
    sb_queries: int
    sb_keys: int
    gla_rows: int
    ssd_rows: int


def _tiles(tokens, seq):
    rows = lambda cap: min(cap, tokens)
    steps = lambda cap: min(cap, seq)
    tiles = _Tiles(proj_rows=rows(1024), proj_cols=1024, out_rows=rows(512), mlp_rows=rows(1024),
                   mlp_hidden=512, ple_rows=rows(512), ssd_out_rows=rows(256), ssd_out_k=1024,
                   sb_queries=steps(512), sb_keys=256, gla_rows=steps(512), ssd_rows=steps(1024))
    assert all(tokens % r == 0 for r in (tiles.proj_rows, tiles.out_rows, tiles.mlp_rows,
                                         tiles.ple_rows, tiles.ssd_out_rows))
    assert seq % tiles.sb_queries == 0 and tiles.sb_queries % tiles.sb_keys == 0
    assert seq % tiles.gla_rows == 0 and tiles.gla_rows % GLA_CHUNK == 0
    assert seq % tiles.ssd_rows == 0 and tiles.ssd_rows % SSD_CHUNK == 0
    return tiles


def _dot(a, b):
    return jnp.dot(a, b, preferred_element_type=F32)


def _dot_nt(a, b):
    return lax.dot_general(a, b, (((1,), (1,)), ((), ())), preferred_element_type=F32)


def _dot_tn(a, b):
    return lax.dot_general(a, b, (((0,), (0,)), ((), ())), preferred_element_type=F32)


def _rms(x, g):
    ms = jnp.mean(x * x, axis=-1, keepdims=True)
    return x * lax.rsqrt(ms + EPS) * g


def _softplus_neg_abs(z):
    return jnp.log(1.0 + jnp.exp(-jnp.abs(z)))


def _log_sigmoid(z):
    return jnp.minimum(z, 0.0) - _softplus_neg_abs(z)


def _silu(z):
    return z * (1.0 / (1.0 + jnp.exp(-z)))


def _split3(x):
    hi = x.astype(BF16)
    r = x - hi.astype(F32)
    mid = r.astype(BF16)
    lo = (r - mid.astype(F32)).astype(BF16)
    return hi, mid, lo


def _dot_exact_rhs(m01, x):
    return _dot(jnp.concatenate([m01] * 3, axis=1), jnp.concatenate(_split3(x), axis=0))


def _dot_split_lhs(x, m01, pieces):
    parts = []
    for _ in range(pieces):
        part = x.astype(BF16)
        x = x - part.astype(F32)
        parts.append(part)
    return _dot(jnp.concatenate(parts, axis=1), jnp.concatenate([m01] * pieces, axis=0))


def _ab_inproj_kernel(x_ref, g_ref, w_ref, wglr_ref, wgu_ref, bg_ref, proj_ref, loga_ref, xn_ref):
    @pl.when(pl.program_id(1) == 0)
    def _():
        xb = _rms(x_ref[...], g_ref[...]).astype(BF16)
        xn_ref[...] = xb
        glr = _dot(xb, wglr_ref[...])
        ga = _dot(glr.astype(BF16), wgu_ref[...]) + bg_ref[...]
        loga_ref[...] = _log_sigmoid(ga) * (1.0 / GLA_TAU)

    proj_ref[...] = _dot(xn_ref[...], w_ref[...]).astype(BF16)


def _ab_inproj(x, g, w, wglr, wgu, bg, *, tm, tn):
    t, d = x.shape
    n = w.shape[1]
    gk = wgu.shape[1]
    return pl.pallas_call(
        _ab_inproj_kernel,
        out_shape=(jax.ShapeDtypeStruct((t, n), BF16), jax.ShapeDtypeStruct((t, gk), F32)),
        grid=(t // tm, n // tn),
        in_specs=[
            pl.BlockSpec((tm, d), lambda i, j: (i, 0)),
            pl.BlockSpec((1, d), lambda i, j: (0, 0)),
            pl.BlockSpec((d, tn), lambda i, j: (0, j)),
            pl.BlockSpec(wglr.shape, lambda i, j: (0, 0)),
            pl.BlockSpec(wgu.shape, lambda i, j: (0, 0)),
            pl.BlockSpec((1, gk), lambda i, j: (0, 0)),
        ],
        out_specs=(pl.BlockSpec((tm, tn), lambda i, j: (i, j)),
                   pl.BlockSpec((tm, gk), lambda i, j: (i, 0))),
        scratch_shapes=[pltpu.VMEM((tm, d), BF16)],
        compiler_params=_params("parallel", "arbitrary"),
        name="ab_inproj",
    )(x, g, w, wglr, wgu, bg)


def _ssd_inproj_kernel(x_ref, g_ref, w_ref, wdt_ref, proj_ref, dt_ref, xn_ref):
    @pl.when(pl.program_id(1) == 0)
    def _():
        xb = _rms(x_ref[...], g_ref[...]).astype(BF16)
        xn_ref[...] = xb
        dt_ref[...] = _dot(xb, wdt_ref[...])

    proj_ref[...] = _dot(xn_ref[...], w_ref[...]).astype(BF16)


def _ssd_inproj(x, g, w, wdt, *, tm, tn):
    t, d = x.shape
    n = w.shape[1]
    nd = wdt.shape[1]
    return pl.pallas_call(
        _ssd_inproj_kernel,
        out_shape=(jax.ShapeDtypeStruct((t, n), BF16), jax.ShapeDtypeStruct((t, nd), F32)),
        grid=(t // tm, n // tn),
        in_specs=[
            pl.BlockSpec((tm, d), lambda i, j: (i, 0)),
            pl.BlockSpec((1, d), lambda i, j: (0, 0)),
            pl.BlockSpec((d, tn), lambda i, j: (0, j)),
            pl.BlockSpec((d, nd), lambda i, j: (0, 0)),
        ],
        out_specs=(pl.BlockSpec((tm, tn), lambda i, j: (i, j)),
                   pl.BlockSpec((tm, nd), lambda i, j: (i, 0))),
        scratch_shapes=[pltpu.VMEM((tm, d), BF16)],
        compiler_params=_params("parallel", "arbitrary"),
        name="ssd_inproj",
    )(x, g, w, wdt)


def _sb_kernel(q_ref, k_ref, v_ref, o_ref, z0_ref, z1_ref, w0_ref, w1_ref, acc_ref, carry_ref,
               *, bq, bk, scale):
    i = pl.program_id(2)
    nd = bq // bk
    q = q_ref[...]
    key = lax.broadcasted_iota(jnp.int32, (bk, bq), 0)
    qry = lax.broadcasted_iota(jnp.int32, (bk, bq), 1)
    col = lax.broadcasted_iota(jnp.int32, (bk, 2 * bk), 1)
    row = lax.broadcasted_iota(jnp.int32, (bk, 2 * bk), 0)
    later2 = jnp.where(jnp.where(col >= bk, col - bk, col) > row, 1.0, 0.0).astype(BF16)

    def scores(start):
        return _dot_nt(k_ref[pl.ds(pl.multiple_of(start, bq), bq), :], q)

    def weighted_values(start, w):
        return _dot_tn(v_ref[pl.ds(pl.multiple_of(start, bq), bq), :], w)

    def block(z, carry, diagonal):
        ws = [None] * nd
        for r in reversed(range(nd)):
            z2 = z[r * bk:(r + 1) * bk, :] * (scale * LOG2E)
            pos = jnp.minimum(z2, 0.0)
            neg = pos - z2
            sp = jnp.log2(1.0 + jnp.exp2(neg + pos))
            log_beta = pos - sp
            l1 = neg - sp
            if diagonal:
                causal = key + r * bk < qry
                l1 = jnp.where(causal, l1, 0.0)
            hi = l1.astype(BF16)
            lo = (l1 - hi.astype(F32)).astype(BF16)
            within = _dot(later2, jnp.concatenate([hi, lo], axis=0))
            w = jnp.exp2(log_beta + (within + carry))
            if diagonal:
                w = jnp.where(causal, w, 0.0)
            ws[r] = w.astype(BF16)
            carry = carry + (within[0:1, :] + l1[0:1, :])
        return carry, jnp.concatenate(ws, axis=0)

    def step(m, z_cur, z_next, w_prev, w_cur):
        start = (i - 1 - m) * bq
        z_next[...] = scores(jnp.maximum(start - bq, 0))
        acc_ref[...] += weighted_values(start + bq, w_prev[...])
        carry, w = block(z_cur[...], carry_ref[...], False)
        w_cur[...] = w
        carry_ref[...] = carry

    carry, w = block(scores(i * bq), jnp.zeros((1, bq), F32), True)
    carry_ref[...] = carry
    w1_ref[...] = w
    acc_ref[...] = jnp.zeros_like(acc_ref)
    z0_ref[...] = scores(jnp.maximum(i - 1, 0) * bq)

    def two_steps(mm, _):
        step(2 * mm, z0_ref, z1_ref, w1_ref, w0_ref)
        step(2 * mm + 1, z1_ref, z0_ref, w0_ref, w1_ref)
        return 0

    lax.fori_loop(0, i // 2, two_steps, 0)
    odd = i % 2 == 1

    @pl.when(odd)
    def _():
        step(i - 1, z0_ref, z1_ref, w1_ref, w0_ref)
        acc_ref[...] += weighted_values(0, w0_ref[...])

    @pl.when(jnp.logical_not(odd))
    def _():
        acc_ref[...] += weighted_values(0, w1_ref[...])

    o_ref[...] = acc_ref[...].T.astype(BF16)


def _sb_attention(proj, *, batch, seq, bq, bk):
    t = proj.shape[0]
    nq = seq // bq
    kernel = functools.partial(_sb_kernel, bq=bq, bk=bk, scale=SB_DIM ** -0.5)
    return pl.pallas_call(
        kernel,
        out_shape=jax.ShapeDtypeStruct((t, SB_HEADS * SB_DIM), BF16),
        grid=(batch, SB_HEADS, nq),
        in_specs=[
            pl.BlockSpec((bq, SB_DIM), lambda b, h, i: (b * nq + i, h)),
            pl.BlockSpec((seq, SB_DIM), lambda b, h, i: (b, SB_HEADS + h)),
            pl.BlockSpec((seq, SB_DIM), lambda b, h, i: (b, 2 * SB_HEADS + h)),
        ],
        out_specs=pl.BlockSpec((bq, SB_DIM), lambda b, h, i: (b * nq + i, h)),
        scratch_shapes=[
            pltpu.VMEM((bq, bq), F32), pltpu.VMEM((bq, bq), F32),
            pltpu.VMEM((bq, bq), BF16), pltpu.VMEM((bq, bq), BF16),
            pltpu.VMEM((SB_DIM, bq), F32),
            pltpu.VMEM((1, bq), F32),
        ],
        compiler_params=_params("parallel", "parallel", "arbitrary"),
        name="sb_attention",
    )(proj, proj, proj)


def _gla_kernel(q_ref, k_ref, v_ref, go_ref, la_ref, gn_ref, o_ref, st_ref, *, nchunks):
    c = GLA_CHUNK

    @pl.when(pl.program_id(2) == 0)
    def _():
        st_ref[...] = jnp.zeros_like(st_ref)

    kw, vw = 2 * GLA_DK, 2 * GLA_DV
    row = lax.broadcasted_iota(jnp.int32, (c, 3 * c), 0)
    col = lax.broadcasted_iota(jnp.int32, (c, 3 * c), 1)
    incl3 = jnp.where(col % c <= row, 1.0, 0.0).astype(BF16)
    srow = lax.broadcasted_iota(jnp.int32, (c, kw), 0)
    scol = lax.broadcasted_iota(jnp.int32, (c, kw), 1)
    causal2 = scol % c <= srow
    krow = lax.broadcasted_iota(jnp.int32, (kw, kw), 0)
    kcol = lax.broadcasted_iota(jnp.int32, (kw, kw), 1)
    key_head = (krow < c) == (kcol < GLA_DK)
    vrow = lax.broadcasted_iota(jnp.int32, (kw, vw), 0)
    vcol = lax.broadcasted_iota(jnp.int32, (kw, vw), 1)
    val_head = (vrow < c) == (vcol < GLA_DV)
    trow = lax.broadcasted_iota(jnp.int32, (vw, kw), 0)
    tcol = lax.broadcasted_iota(jnp.int32, (vw, kw), 1)
    same_head = (trow < GLA_DV) == (tcol < GLA_DK)
    gn = gn_ref[...]
    st = st_ref[...]

    for n in range(nchunks):
        sl = pl.ds(n * c, c)
        gcum = _dot(incl3, jnp.concatenate(_split3(la_ref[sl, :]), axis=0))
        glast = gcum[c - 1:c, :]
        q = q_ref[sl, :].astype(F32) * (GLA_DK ** -0.5)
        k = k_ref[sl, :].astype(F32)
        v = v_ref[sl, :]
        q_dec = (q * jnp.exp(gcum)).astype(BF16)
        k_inv = (k * jnp.exp(-gcum)).astype(BF16)
        k_end = (k * jnp.exp(glast - gcum)).astype(BF16)
        k2 = jnp.where(key_head, jnp.concatenate([k_inv, k_inv], axis=0), jnp.zeros((kw, kw), BF16))
        scores = jnp.where(causal2, _dot_nt(q_dec, k2), 0.0)
        v2 = jnp.where(val_head, jnp.concatenate([v, v], axis=0), jnp.zeros((kw, vw), BF16))
        o = _dot(scores.astype(BF16), v2) + _dot_nt(q_dec, st.astype(BF16))
        st = st * jnp.exp(glast) + jnp.where(same_head, _dot_tn(v, k_end), 0.0)
        gate = _silu(go_ref[sl, :].astype(F32))
        outs = []
        for a in range(2):
            hs = slice(a * GLA_DV, (a + 1) * GLA_DV)
            outs.append(_rms(o[:, hs], gn) * gate[:, hs])
        o_ref[sl, :] = jnp.concatenate(outs, axis=1).astype(BF16)
    st_ref[...] = st


def _gla(proj, log_a, gnorm, *, batch, seq, ts):
    t = proj.shape[0]
    nt = seq // ts
    pairs = GLA_HEADS // 2
    kw, vw = 2 * GLA_DK, 2 * GLA_DV
    q0 = 3 * SB_HEADS * SB_DIM // kw
    k0 = q0 + GLA_HEADS * GLA_DK // kw
    v0 = (3 * SB_HEADS * SB_DIM + 2 * GLA_HEADS * GLA_DK) // vw
    g0 = v0 + GLA_HEADS * GLA_DV // vw
    kernel = functools.partial(_gla_kernel, nchunks=ts // GLA_CHUNK)
    return pl.pallas_call(
        kernel,
        out_shape=jax.ShapeDtypeStruct((t, GLA_HEADS * GLA_DV), BF16),
        grid=(batch, pairs, nt),
        in_specs=[
            pl.BlockSpec((ts, kw), lambda b, p, i: (b * nt + i, q0 + p)),
            pl.BlockSpec((ts, kw), lambda b, p, i: (b * nt + i, k0 + p)),
            pl.BlockSpec((ts, vw), lambda b, p, i: (b * nt + i, v0 + p)),
            pl.BlockSpec((ts, vw), lambda b, p, i: (b * nt + i, g0 + p)),
            pl.BlockSpec((ts, kw), lambda b, p, i: (b * nt + i, p)),
            pl.BlockSpec((1, GLA_DV), lambda b, p, i: (0, 0)),
        ],
        out_specs=pl.BlockSpec((ts, vw), lambda b, p, i: (b * nt + i, p)),
        scratch_shapes=[pltpu.VMEM((vw, kw), F32)],
        compiler_params=_params("parallel", "parallel", "arbitrary"),
        name="gla",
    )(proj, proj, proj, proj, log_a, gnorm)


def _ab_outproj_kernel(h_ref, a_ref, b_ref, w_ref, o_ref):
    ka = a_ref.shape[1]
    acc = _dot(a_ref[...], w_ref[0:ka, :]) + _dot(b_ref[...], w_ref[ka:, :])
    o_ref[...] = h_ref[...] + acc


def _ab_outproj(h, a, b, w, *, tm):
    t, d = h.shape
    return pl.pallas_call(
        _ab_outproj_kernel,
        out_shape=jax.ShapeDtypeStruct((t, d), F32),
        grid=(t // tm,),
        in_specs=[
            pl.BlockSpec((tm, d), lambda i: (i, 0)),
            pl.BlockSpec((tm, a.shape[1]), lambda i: (i, 0)),
            pl.BlockSpec((tm, b.shape[1]), lambda i: (i, 0)),
            pl.BlockSpec(w.shape, lambda i: (0, 0)),
        ],
        out_specs=pl.BlockSpec((tm, d), lambda i: (i, 0)),
        compiler_params=_params("parallel"),
        name="ab_outproj",
    )(h, a, b, w)


def _mlp_kernel(h_ref, g_ref, wu_ref, wd_ref, o_ref, xn_ref):
    @pl.when(pl.program_id(1) == 0)
    def _():
        h = h_ref[...]
        xn_ref[...] = _rms(h, g_ref[...]).astype(BF16)
        o_ref[...] = h

    u = jnp.maximum(_dot(xn_ref[...], wu_ref[...]), 0.0)
    o_ref[...] += _dot((u * u).astype(BF16), wd_ref[...])


def _mlp(h, g, wu, wd, layer, *, tm, tf):
    t, d = h.shape
    f = wu.shape[2]
    return pl.pallas_call(
        _mlp_kernel,
        out_shape=jax.ShapeDtypeStruct((t, d), F32),
        grid=(t // tm, f // tf),
        in_specs=[
            pl.BlockSpec((tm, d), lambda i, j: (i, 0)),
            pl.BlockSpec((None, 1, d), lambda i, j: (layer, 0, 0)),
            pl.BlockSpec((None, d, tf), lambda i, j: (layer, 0, j)),
            pl.BlockSpec((None, tf, d), lambda i, j: (layer, j, 0)),
        ],
        out_specs=pl.BlockSpec((tm, d), lambda i, j: (i, 0)),
        scratch_shapes=[pltpu.VMEM((tm, d), BF16)],
        compiler_params=_params("parallel", "arbitrary"),
        name="mlp",
    )(h, g, wu, wd)


def _ple_kernel(h_ref, p_ref, wg_ref, wp_ref, fn_ref, o_ref, *, final_norm):
    h = h_ref[...]
    gate = _dot(h.astype(BF16), wg_ref[...])
    emb = _dot(p_ref[...].astype(BF16), wp_ref[...])
    out = h + emb * (1.0 / (1.0 + jnp.exp(-gate)))
    if final_norm:
        out = _rms(out, fn_ref[...])
    o_ref[...] = out


def _ple(h, p, wg, wp, fn, layer, *, tm, final_norm):
    t, d = h.shape
    e = p.shape[2]
    kernel = functools.partial(_ple_kernel, final_norm=final_norm)
    return pl.pallas_call(
        kernel,
        out_shape=jax.ShapeDtypeStruct((t, d), F32),
        grid=(t // tm,),
        in_specs=[
            pl.BlockSpec((tm, d), lambda i: (i, 0)),
            pl.BlockSpec((None, tm, e), lambda i: (layer, i, 0)),
            pl.BlockSpec((None, d, d), lambda i: (layer, 0, 0), pipeline_mode=pl.Buffered(1)),
            pl.BlockSpec((None, e, d), lambda i: (layer, 0, 0), pipeline_mode=pl.Buffered(1)),
            pl.BlockSpec((1, d), lambda i: (0, 0)),
        ],
        out_specs=pl.BlockSpec((tm, d), lambda i: (i, 0)),
        compiler_params=_params("parallel"),
        name="ple",
    )(h, p, wg, wp, fn)


def _ssd_kernel(xbc_ref, par_ref, dt_ref, sel_ref, y_ref, st_ref, pad_ref, *, nchunks, heads):
    c = SSD_CHUNK
    ts = xbc_ref.shape[0]
    gw = heads * SSD_HEAD_DIM
    n = SSD_STATE
    first = pl.program_id(2) == 0

    @pl.when(first)
    def _():
        st_ref[...] = jnp.zeros_like(st_ref)
        pad_ref[0:CONV_HALO, :] = jnp.zeros((CONV_HALO, pad_ref.shape[1]), F32)

    pad_ref[CONV_HALO:CONV_HALO + ts, :] = xbc_ref[...].astype(F32)
    conv = par_ref[0, 0:1, :] + par_ref[0, SSD_CONV:SSD_CONV + 1, :] * pad_ref[CONV_HALO:CONV_HALO + ts, :]
    for back in range(1, SSD_CONV):
        tap = SSD_CONV - back
        conv = conv + par_ref[0, tap:tap + 1, :] * pad_ref[CONV_HALO - back:CONV_HALO - back + ts, :]
    pad_ref[0:CONV_HALO, :] = pad_ref[ts:ts + CONV_HALO, :]
    xbc = _silu(conv)
    xs_all = xbc[:, :gw]
    bm_all = xbc[:, gw:gw + n].astype(BF16)
    cm_all = xbc[:, gw + n:].astype(BF16)

    row = lax.broadcasted_iota(jnp.int32, (c, c), 0)
    col = lax.broadcasted_iota(jnp.int32, (c, c), 1)
    tril = col <= row
    incl = jnp.where(tril, 1.0, 0.0).astype(BF16)
    lane = lax.broadcasted_iota(jnp.int32, (1, LANES), 1)
    lo_half = lane < SSD_HEAD_DIM
    sel = sel_ref[0]
    dsk = par_ref[0, 5:6, 0:gw]
    dt_bias = par_ref[0, 6:7, 0:LANES]
    a2 = -jnp.exp(par_ref[0, 7:8, 0:LANES]) * LOG2E
    st = st_ref[...]

    for ci in range(nchunks):
        sl = slice(ci * c, (ci + 1) * c)
        xs = xs_all[sl, :]
        bm = bm_all[sl, :]
        cm = cm_all[sl, :]
        z = _dot_split_lhs(dt_ref[sl, :], sel, 2) + dt_bias
        dt = jnp.maximum(z, 0.0) + _softplus_neg_abs(z)
        a_cs = _dot_exact_rhs(incl, dt * a2)
        a_cs_t = a_cs.T
        a_last = a_cs[c - 1:c, :]
        cb = _dot_nt(cm, bm)
        y_pairs, ea_pairs, xds_pairs = [], [], []
        for pr in range(heads // 2):
            cols, ms = [], []
            for j in (2 * pr, 2 * pr + 1):
                a_col = jnp.broadcast_to(a_cs[:, j:j + 1], (c, LANES))
                decay = jnp.where(tril, jnp.exp2(a_col - a_cs_t[j:j + 1, :]), 0.0)
                ms.append((cb * decay).astype(BF16))
                cols.append((a_col, jnp.broadcast_to(dt[:, j:j + 1], (c, LANES)), a_last[:, j:j + 1]))
            pick = lambda f: jnp.where(lo_half, f(cols[0]), f(cols[1]))
            xp = xs[:, pr * LANES:(pr + 1) * LANES] * pick(lambda v: v[1])
            ea_pairs.append(pick(lambda v: jnp.exp2(v[0])))
            xds_pairs.append(xp * pick(lambda v: jnp.exp2(v[2] - v[0])))
            lhs = jnp.concatenate(ms, axis=1)
            rhs = jnp.concatenate([jnp.where(lo_half, xp, 0.0), jnp.where(lo_half, 0.0, xp)],
                                  axis=0).astype(BF16)
            y_pairs.append(_dot(lhs, rhs))
        y_diag = jnp.concatenate(y_pairs, axis=1)
        ea_x = jnp.concatenate(ea_pairs, axis=1)
        y_off = _dot(cm, st.astype(BF16)) * ea_x
        st = st * ea_x[c - 1:c, :] + _dot_tn(bm, jnp.concatenate(xds_pairs, axis=1).astype(BF16))
        y_ref[sl, :] = (y_diag + y_off + dsk * xs).astype(BF16)
    st_ref[...] = st


def _ssd(proj, dt_raw, params, *, batch, seq, ts, inner):
    t = proj.shape[0]
    nt = seq // ts
    g = SSD_GROUPS
    gw = inner // g
    heads = gw // SSD_HEAD_DIM
    n = SSD_STATE
    cw = gw + 2 * n

    sel = np.zeros((g, LANES, LANES), np.float32)
    for j in range(heads):
        for gi in range(g):
            sel[gi, gi * heads + j, j] = 1.0
    sel = jnp.asarray(sel, BF16)

    kernel = functools.partial(_ssd_kernel, nchunks=ts // SSD_CHUNK, heads=heads)
    row = lambda b, gi, i: b * nt + i
    return pl.pallas_call(
        kernel,
        out_shape=jax.ShapeDtypeStruct((t, inner), BF16),
        grid=(batch, g, nt),
        in_specs=[
            pl.BlockSpec((ts, cw), lambda b, gi, i: (row(b, gi, i), gi)),
            pl.BlockSpec((1, 8, cw), lambda b, gi, i: (gi, 0, 0)),
            pl.BlockSpec((ts, LANES), lambda b, gi, i: (row(b, gi, i), 0)),
            pl.BlockSpec((1, LANES, LANES), lambda b, gi, i: (gi, 0, 0)),
        ],
        out_specs=pl.BlockSpec((ts, gw), lambda b, gi, i: (row(b, gi, i), gi)),
        scratch_shapes=[
            pltpu.VMEM((n, gw), F32),
            pltpu.VMEM((ts + CONV_HALO, cw), F32),
        ],
        compiler_params=_params("parallel", "parallel", "arbitrary"),
        name="ssd",
    )(proj, params, dt_raw, sel)


def _ssd_outproj_kernel(h_ref, y_ref, zlo_ref, zhi_ref, g_ref, w_ref, o_ref, *, kc):
    inner = y_ref.shape[1]
    half = zlo_ref.shape[1]
    ssq = None
    acc = None
    for k0 in range(0, inner, kc):
        z_ref, zoff = (zlo_ref, k0) if k0 < half else (zhi_ref, k0 - half)
        z = z_ref[:, zoff:zoff + kc].astype(F32)
        yg = y_ref[:, k0:k0 + kc].astype(F32) * _silu(z)
        part = jnp.sum(yg * yg, axis=-1, keepdims=True)
        ssq = part if ssq is None else ssq + part
        term = _dot((yg * g_ref[:, k0:k0 + kc]).astype(BF16), w_ref[k0:k0 + kc, :])
        acc = term if acc is None else acc + term
    o_ref[...] = h_ref[...] + acc * lax.rsqrt(ssq * (1.0 / inner) + EPS)


def _ssd_outproj(h, y, proj, g, w, *, tm, kc):
    t, d = h.shape
    inner = y.shape[1]
    half = inner // 2
    z0 = (proj.shape[1] - inner) // half
    kernel = functools.partial(_ssd_outproj_kernel, kc=kc)
    return pl.pallas_call(
        kernel,
        out_shape=jax.ShapeDtypeStruct((t, d), F32),
        grid=(t // tm,),
        in_specs=[
            pl.BlockSpec((tm, d), lambda i: (i, 0)),
            pl.BlockSpec((tm, inner), lambda i: (i, 0)),
            pl.BlockSpec((tm, half), lambda i: (i, z0)),
            pl.BlockSpec((tm, half), lambda i: (i, z0 + 1)),
            pl.BlockSpec((1, inner), lambda i: (0, 0)),
            pl.BlockSpec((inner, d), lambda i: (0, 0), pipeline_mode=pl.Buffered(1)),
        ],
        out_specs=pl.BlockSpec((tm, d), lambda i: (i, 0)),
        compiler_params=_params("parallel"),
        name="ssd_outproj",
    )(h, y, proj, proj, g, w)


def _pad_cols(w, n):
    return jnp.pad(w, ((0, 0), (0, n - w.shape[1])))


def _group_cols(a, inner):
    g, n = SSD_GROUPS, SSD_STATE
    gw = inner // g
    b0, c0 = inner, inner + g * n
    out = []
    for gi in range(g):
        out += [a[:, gi * gw:(gi + 1) * gw], a[:, b0 + gi * n:b0 + (gi + 1) * n],
                a[:, c0 + gi * n:c0 + (gi + 1) * n]]
    return out


def _ssd_params(conv_w, conv_b, d_skip, dt_bias, a_log, inner):
    g = SSD_GROUPS
    cw = inner // g + 2 * SSD_STATE
    grouped = jnp.concatenate(_group_cols(jnp.concatenate([conv_b[None, :], conv_w], axis=0), inner),
                              axis=1)
    grouped = grouped.reshape(1 + SSD_CONV, g, cw).transpose(1, 0, 2)
    per_head = lambda v, rep: _pad_cols(jnp.repeat(v, rep).reshape(g, -1), cw)[:, None, :]
    return jnp.concatenate([grouped, per_head(d_skip, SSD_HEAD_DIM), per_head(dt_bias, 1),
                            per_head(a_log, 1)], axis=1).astype(F32)


def kernel(x, p, norm_mix, norm_mlp, ab_w_in, ab_w_gate_up, ab_b_gate, ab_gla_norm, ab_w_out,
           ssd_w_in, ssd_conv_w, ssd_conv_b, ssd_dt_bias, ssd_a_log, ssd_d, ssd_norm, ssd_w_out,
           mlp_w_up, mlp_w_down, ple_w_proj, ple_w_gate, final_norm):
    batch, seq, d = x.shape
    t = batch * seq
    depth = p.shape[0]
    tiles = _tiles(t, seq)
    h = x.reshape(t, d)
    row = lambda v: v.reshape(1, -1).astype(F32)
    mlp_norm = norm_mlp.reshape(depth, 1, d).astype(F32)
    mlp_up, mlp_down = mlp_w_up.astype(BF16), mlp_w_down.astype(BF16)
    ple_gate, ple_proj = ple_w_gate.astype(BF16), ple_w_proj.astype(BF16)
    p_rows = p.reshape(depth, t, -1)

    for layer in range(depth):
        j = layer // 2
        if layer % 2 == 0:
            w = ab_w_in[j].astype(BF16)
            glr0 = 3 * SB_HEADS * SB_DIM + 2 * GLA_HEADS * GLA_DK + GLA_HEADS * GLA_DV
            w_main = jnp.concatenate([w[:, :glr0], w[:, glr0 + GLA_RANK:]], axis=1)
            w_glr = _pad_cols(w[:, glr0:glr0 + GLA_RANK], LANES)
            w_gu = jnp.pad(ab_w_gate_up[j], ((0, LANES - GLA_RANK), (0, 0))).astype(BF16)
            proj, log_a = _ab_inproj(h, row(norm_mix[layer]), w_main, w_glr, w_gu,
                                     row(ab_b_gate[j]), tm=tiles.proj_rows, tn=tiles.proj_cols)
            o_sb = _sb_attention(proj, batch=batch, seq=seq, bq=tiles.sb_queries, bk=tiles.sb_keys)
            o_gla = _gla(proj, log_a, row(ab_gla_norm[j]), batch=batch, seq=seq, ts=tiles.gla_rows)
            h = _ab_outproj(h, o_sb, o_gla, ab_w_out[j].astype(BF16), tm=tiles.out_rows)
        else:
            w = ssd_w_in[j].astype(BF16)
            inner = ssd_w_out.shape[1]
            n_zx = w.shape[1] - inner // SSD_HEAD_DIM
            w_perm = jnp.concatenate(_group_cols(w[:, inner:n_zx], inner) + [w[:, :inner]], axis=1)
            proj, dt_raw = _ssd_inproj(h, row(norm_mix[layer]), w_perm, _pad_cols(w[:, n_zx:], LANES),
                                       tm=tiles.proj_rows, tn=tiles.proj_cols)
            params = _ssd_params(ssd_conv_w[j], ssd_conv_b[j], ssd_d[j], ssd_dt_bias[j],
                                 ssd_a_log[j], inner)
            y = _ssd(proj, dt_raw, params, batch=batch, seq=seq, ts=tiles.ssd_rows, inner=inner)
            h = _ssd_outproj(h, y, proj, row(ssd_norm[j]), ssd_w_out[j].astype(BF16),
                             tm=tiles.ssd_out_rows, kc=tiles.ssd_out_k)
        h = _mlp(h, mlp_norm, mlp_up, mlp_down, layer, tm=tiles.mlp_rows, tf=tiles.mlp_hidden)
        h = _ple(h, p_rows, ple_gate, ple_proj, row(final_norm), layer, tm=tiles.ple_rows,
                 final_norm=layer == depth - 1)
    return h.reshape(batch, seq, d)
```

```python
import functools
from typing import NamedTuple

import numpy as np
import jax
import jax.numpy as jnp
from jax import lax
from jax.experimental import pallas as pl
from jax.experimental.pallas import tpu as pltpu

F32 = jnp.float32
BF16 = jnp.bfloat16

EPS = 1e-6
LOG2E = 1.4426950408889634
LANES = 128
VMEM_LIMIT = 56 * 1024 * 1024

SB_HEADS = 8
SB_DIM = 128
GLA_HEADS = 8
GLA_DK = 64
GLA_DV = 128
GLA_RANK = 16
GLA_TAU = 16.0
GLA_CHUNK = 64
SSD_HEAD_DIM = 64
SSD_GROUPS = 8
SSD_STATE = 128
SSD_CONV = 4
SSD_CHUNK = 128
CONV_HALO = 8


def _params(*sem):
    return pltpu.CompilerParams(dimension_semantics=sem, vmem_limit_bytes=VMEM_LIMIT)


class _Tiles(NamedTuple):
    proj_rows: int
    proj_cols: int
    out_rows: int
    mlp_rows: int
    mlp_hidden: int
    ple_rows: int
    ssd_out_rows: int
    ssd_out_k: int
    sb_queries: int
    sb_keys: int
    gla_rows: int
    ssd_rows: int


def _tiles(tokens, seq):
    rows = lambda cap: min(cap, tokens)
    steps = lambda cap: min(cap, seq)
    tiles = _Tiles(proj_rows=rows(1024), proj_cols=1024, out_rows=rows(512), mlp_rows=rows(1024),
                   mlp_hidden=512, ple_rows=rows(512), ssd_out_rows=rows(256), ssd_out_k=256,
                   sb_queries=steps(512), sb_keys=256, gla_rows=steps(1024), ssd_rows=steps(1024))
    assert all(tokens % r == 0 for r in (tiles.proj_rows, tiles.out_rows, tiles.mlp_rows,
                                         tiles.ple_rows, tiles.ssd_out_rows))
    assert seq % tiles.sb_queries == 0 and tiles.sb_queries % tiles.sb_keys == 0
    assert seq % tiles.gla_rows == 0 and tiles.gla_rows % GLA_CHUNK == 0
    assert seq % tiles.ssd_rows == 0 and tiles.ssd_rows % SSD_CHUNK == 0
    return tiles


def _dot(a, b):
    return jnp.dot(a, b, preferred_element_type=F32)


def _dot_nt(a, b):
    return lax.dot_general(a, b, (((1,), (1,)), ((), ())), preferred_element_type=F32)


def _dot_tn(a, b):
    return lax.dot_general(a, b, (((0,), (0,)), ((), ())), preferred_element_type=F32)


def _rms(x, g):
    ms = jnp.mean(x * x, axis=-1, keepdims=True)
    return x * lax.rsqrt(ms + EPS) * g


def _softplus_neg_abs(z):
    return jnp.log(1.0 + jnp.exp(-jnp.abs(z)))


def _log_sigmoid(z):
    return jnp.minimum(z, 0.0) - _softplus_neg_abs(z)


def _silu(z):
    return z * (1.0 / (1.0 + jnp.exp(-z)))


def _split3(x):
    hi = x.astype(BF16)
    r = x - hi.astype(F32)
    mid = r.astype(BF16)
    lo = (r - mid.astype(F32)).astype(BF16)
    return hi, mid, lo


def _dot_exact_rhs(m01, x):
    return _dot(jnp.concatenate([m01] * 3, axis=1), jnp.concatenate(_split3(x), axis=0))


def _dot_split_lhs(x, m01, pieces):
    parts = []
    for _ in range(pieces):
        part = x.astype(BF16)
        x = x - part.astype(F32)
        parts.append(part)
    return _dot(jnp.concatenate(parts, axis=1), jnp.concatenate([m01] * pieces, axis=0))


def _ab_inproj_kernel(x_ref, g_ref, w_ref, wglr_ref, wgu_ref, bg_ref, proj_ref, loga_ref, xn_ref):
    @pl.when(pl.program_id(1) == 0)
    def _():
        xb = _rms(x_ref[...], g_ref[...]).astype(BF16)
        xn_ref[...] = xb
        glr = _dot(xb, wglr_ref[...])
        ga = _dot(glr.astype(BF16), wgu_ref[...]) + bg_ref[...]
        loga_ref[...] = _log_sigmoid(ga) * (1.0 / GLA_TAU)

    proj_ref[...] = _dot(xn_ref[...], w_ref[...]).astype(BF16)


def _ab_inproj(x, g, w, wglr, wgu, bg, *, tm, tn):
    t, d = x.shape
    n = w.shape[1]
    gk = wgu.shape[1]
    return pl.pallas_call(
        _ab_inproj_kernel,
        out_shape=(jax.ShapeDtypeStruct((t, n), BF16), jax.ShapeDtypeStruct((t, gk), F32)),
        grid=(t // tm, n // tn),
        in_specs=[
            pl.BlockSpec((tm, d), lambda i, j: (i, 0)),
            pl.BlockSpec((1, d), lambda i, j: (0, 0)),
            pl.BlockSpec((d, tn), lambda i, j: (0, j)),
            pl.BlockSpec(wglr.shape, lambda i, j: (0, 0)),
            pl.BlockSpec(wgu.shape, lambda i, j: (0, 0)),
            pl.BlockSpec((1, gk), lambda i, j: (0, 0)),
        ],
        out_specs=(pl.BlockSpec((tm, tn), lambda i, j: (i, j)),
                   pl.BlockSpec((tm, gk), lambda i, j: (i, 0))),
        scratch_shapes=[pltpu.VMEM((tm, d), BF16)],
        compiler_params=_params("parallel", "arbitrary"),
        name="ab_inproj",
    )(x, g, w, wglr, wgu, bg)


def _ssd_inproj_kernel(x_ref, g_ref, w_ref, wdt_ref, proj_ref, dt_ref, xn_ref):
    @pl.when(pl.program_id(1) == 0)
    def _():
        xb = _rms(x_ref[...], g_ref[...]).astype(BF16)
        xn_ref[...] = xb
        dt_ref[...] = _dot(xb, wdt_ref[...])

    proj_ref[...] = _dot(xn_ref[...], w_ref[...]).astype(BF16)


def _ssd_inproj(x, g, w, wdt, *, tm, tn):
    t, d = x.shape
    n = w.shape[1]
    nd = wdt.shape[1]
    return pl.pallas_call(
        _ssd_inproj_kernel,
        out_shape=(jax.ShapeDtypeStruct((t, n), BF16), jax.ShapeDtypeStruct((t, nd), F32)),
        grid=(t // tm, n // tn),
        in_specs=[
            pl.BlockSpec((tm, d), lambda i, j: (i, 0)),
            pl.BlockSpec((1, d), lambda i, j: (0, 0)),
            pl.BlockSpec((d, tn), lambda i, j: (0, j)),
            pl.BlockSpec((d, nd), lambda i, j: (0, 0)),
        ],
        out_specs=(pl.BlockSpec((tm, tn), lambda i, j: (i, j)),
                   pl.BlockSpec((tm, nd), lambda i, j: (i, 0))),
        scratch_shapes=[pltpu.VMEM((tm, d), BF16)],
        compiler_params=_params("parallel", "arbitrary"),
        name="ssd_inproj",
    )(x, g, w, wdt)


def _sb_kernel(q_ref, k_ref, v_ref, o_ref, z0_ref, z1_ref, w0_ref, w1_ref, acc_ref, carry_ref,
               *, bq, bk, scale):
    i = pl.program_id(2)
    nd = bq // bk
    q = q_ref[...]
    key = lax.broadcasted_iota(jnp.int32, (bk, bq), 0)
    qry = lax.broadcasted_iota(jnp.int32, (bk, bq), 1)
    col = lax.broadcasted_iota(jnp.int32, (bk, 2 * bk), 1)
    row = lax.broadcasted_iota(jnp.int32, (bk, 2 * bk), 0)
    later2 = jnp.where(jnp.where(col >= bk, col - bk, col) > row, 1.0, 0.0).astype(BF16)

    def scores(start):
        return _dot_nt(k_ref[pl.ds(pl.multiple_of(start, bq), bq), :], q)

    def weighted_values(start, w):
        return _dot_tn(v_ref[pl.ds(pl.multiple_of(start, bq), bq), :], w)

    def block(z, carry, diagonal):
        ws = [None] * nd
        for r in reversed(range(nd)):
            z2 = z[r * bk:(r + 1) * bk, :] * (scale * LOG2E)
            pos = jnp.minimum(z2, 0.0)
            neg = pos - z2
            sp = jnp.log2(1.0 + jnp.exp2(neg + pos))
            log_beta = pos - sp
            l1 = neg - sp
            if diagonal:
                causal = key + r * bk < qry
                l1 = jnp.where(causal, l1, 0.0)
            hi = l1.astype(BF16)
            lo = (l1 - hi.astype(F32)).astype(BF16)
            within = _dot(later2, jnp.concatenate([hi, lo], axis=0))
            w = jnp.exp2(log_beta + (within + carry))
            if diagonal:
                w = jnp.where(causal, w, 0.0)
            ws[r] = w.astype(BF16)
            carry = carry + (within[0:1, :] + l1[0:1, :])
        return carry, jnp.concatenate(ws, axis=0)

    def step(m, z_cur, z_next, w_prev, w_cur):
        start = (i - 1 - m) * bq
        z_next[...] = scores(jnp.maximum(start - bq, 0))
        acc_ref[...] += weighted_values(start + bq, w_prev[...])
        carry, w = block(z_cur[...], carry_ref[...], False)
        w_cur[...] = w
        carry_ref[...] = carry

    carry, w = block(scores(i * bq), jnp.zeros((1, bq), F32), True)
    carry_ref[...] = carry
    w1_ref[...] = w
    acc_ref[...] = jnp.zeros_like(acc_ref)
    z0_ref[...] = scores(jnp.maximum(i - 1, 0) * bq)

    def two_steps(mm, _):
        step(2 * mm, z0_ref, z1_ref, w1_ref, w0_ref)
        step(2 * mm + 1, z1_ref, z0_ref, w0_ref, w1_ref)
        return 0

    lax.fori_loop(0, i // 2, two_steps, 0)
    odd = i % 2 == 1

    @pl.when(odd)
    def _():
        step(i - 1, z0_ref, z1_ref, w1_ref, w0_ref)
        acc_ref[...] += weighted_values(0, w0_ref[...])

    @pl.when(jnp.logical_not(odd))
    def _():
        acc_ref[...] += weighted_values(0, w1_ref[...])

    o_ref[...] = acc_ref[...].T.astype(BF16)


def _sb_attention(proj, *, batch, seq, bq, bk):
    t = proj.shape[0]
    nq = seq // bq
    kernel = functools.partial(_sb_kernel, bq=bq, bk=bk, scale=SB_DIM ** -0.5)
    return pl.pallas_call(
        kernel,
        out_shape=jax.ShapeDtypeStruct((t, SB_HEADS * SB_DIM), BF16),
        grid=(batch, SB_HEADS, nq),
        in_specs=[
            pl.BlockSpec((bq, SB_DIM), lambda b, h, i: (b * nq + i, h)),
            pl.BlockSpec((seq, SB_DIM), lambda b, h, i: (b, SB_HEADS + h)),
            pl.BlockSpec((seq, SB_DIM), lambda b, h, i: (b, 2 * SB_HEADS + h)),
        ],
        out_specs=pl.BlockSpec((bq, SB_DIM), lambda b, h, i: (b * nq + i, h)),
        scratch_shapes=[
            pltpu.VMEM((bq, bq), F32), pltpu.VMEM((bq, bq), F32),
            pltpu.VMEM((bq, bq), BF16), pltpu.VMEM((bq, bq), BF16),
            pltpu.VMEM((SB_DIM, bq), F32),
            pltpu.VMEM((1, bq), F32),
        ],
        compiler_params=_params("parallel", "parallel", "arbitrary"),
        name="sb_attention",
    )(proj, proj, proj)


def _gla_kernel(q_ref, k_ref, v_ref, go_ref, la_ref, gn_ref, o_ref, st_ref, *, nchunks):
    c = GLA_CHUNK

    @pl.when(pl.program_id(2) == 0)
    def _():
        st_ref[...] = jnp.zeros_like(st_ref)

    kw, vw = 2 * GLA_DK, 2 * GLA_DV
    row = lax.broadcasted_iota(jnp.int32, (c, 3 * c), 0)
    col = lax.broadcasted_iota(jnp.int32, (c, 3 * c), 1)
    incl3 = jnp.where(col % c <= row, 1.0, 0.0).astype(BF16)
    srow = lax.broadcasted_iota(jnp.int32, (c, kw), 0)
    scol = lax.broadcasted_iota(jnp.int32, (c, kw), 1)
    causal2 = scol % c <= srow
    krow = lax.broadcasted_iota(jnp.int32, (kw, kw), 0)
    kcol = lax.broadcasted_iota(jnp.int32, (kw, kw), 1)
    key_head = (krow < c) == (kcol < GLA_DK)
    vrow = lax.broadcasted_iota(jnp.int32, (kw, vw), 0)
    vcol = lax.broadcasted_iota(jnp.int32, (kw, vw), 1)
    val_head = (vrow < c) == (vcol < GLA_DV)
    trow = lax.broadcasted_iota(jnp.int32, (vw, kw), 0)
    tcol = lax.broadcasted_iota(jnp.int32, (vw, kw), 1)
    same_head = (trow < GLA_DV) == (tcol < GLA_DK)
    gn = gn_ref[...]
    st = st_ref[...]

    for n in range(nchunks):
        sl = pl.ds(n * c, c)
        gcum = _dot(incl3, jnp.concatenate(_split3(la_ref[sl, :]), axis=0))
        glast = gcum[c - 1:c, :]
        q = q_ref[sl, :].astype(F32) * (GLA_DK ** -0.5)
        k = k_ref[sl, :].astype(F32)
        v = v_ref[sl, :]
        q_dec = (q * jnp.exp(gcum)).astype(BF16)
        k_inv = (k * jnp.exp(-gcum)).astype(BF16)
        k_end = (k * jnp.exp(glast - gcum)).astype(BF16)
        k2 = jnp.where(key_head, jnp.concatenate([k_inv, k_inv], axis=0), jnp.zeros((kw, kw), BF16))
        scores = jnp.where(causal2, _dot_nt(q_dec, k2), 0.0)
        v2 = jnp.where(val_head, jnp.concatenate([v, v], axis=0), jnp.zeros((kw, vw), BF16))
        o = _dot(scores.astype(BF16), v2) + _dot_nt(q_dec, st.astype(BF16))
        st = st * jnp.exp(glast) + jnp.where(same_head, _dot_tn(v, k_end), 0.0)
        gate = _silu(go_ref[sl, :].astype(F32))
        outs = []
        for a in range(2):
            hs = slice(a * GLA_DV, (a + 1) * GLA_DV)
            outs.append(_rms(o[:, hs], gn) * gate[:, hs])
        o_ref[sl, :] = jnp.concatenate(outs, axis=1).astype(BF16)
    st_ref[...] = st


def _gla(proj, log_a, gnorm, *, batch, seq, ts):
    t = proj.shape[0]
    nt = seq // ts
    pairs = GLA_HEADS // 2
    kw, vw = 2 * GLA_DK, 2 * GLA_DV
    q0 = 3 * SB_HEADS * SB_DIM // kw
    k0 = q0 + GLA_HEADS * GLA_DK // kw
    v0 = (3 * SB_HEADS * SB_DIM + 2 * GLA_HEADS * GLA_DK) // vw
    g0 = v0 + GLA_HEADS * GLA_DV // vw
    kernel = functools.partial(_gla_kernel, nchunks=ts // GLA_CHUNK)
    return pl.pallas_call(
        kernel,
        out_shape=jax.ShapeDtypeStruct((t, GLA_HEADS * GLA_DV), BF16),
        grid=(batch, pairs, nt),
        in_specs=[
            pl.BlockSpec((ts, kw), lambda b, p, i: (b * nt + i, q0 + p)),
            pl.BlockSpec((ts, kw), lambda b, p, i: (b * nt + i, k0 + p)),
            pl.BlockSpec((ts, vw), lambda b, p, i: (b * nt + i, v0 + p)),
            pl.BlockSpec((ts, vw), lambda b, p, i: (b * nt + i, g0 + p)),
            pl.BlockSpec((ts, kw), lambda b, p, i: (b * nt + i, p)),
            pl.BlockSpec((1, GLA_DV), lambda b, p, i: (0, 0)),
        ],
        out_specs=pl.BlockSpec((ts, vw), lambda b, p, i: (b * nt + i, p)),
        scratch_shapes=[pltpu.VMEM((vw, kw), F32)],
        compiler_params=_params("parallel", "parallel", "arbitrary"),
        name="gla",
    )(proj, proj, proj, proj, log_a, gnorm)


def _ab_outproj_kernel(h_ref, a_ref, b_ref, w_ref, o_ref):
    ka = a_ref.shape[1]
    acc = _dot(a_ref[...], w_ref[0:ka, :]) + _dot(b_ref[...], w_ref[ka:, :])
    o_ref[...] = h_ref[...] + acc


def _ab_outproj(h, a, b, w, *, tm):
    t, d = h.shape
    return pl.pallas_call(
        _ab_outproj_kernel,
        out_shape=jax.ShapeDtypeStruct((t, d), F32),
        grid=(t // tm,),
        in_specs=[
            pl.BlockSpec((tm, d), lambda i: (i, 0)),
            pl.BlockSpec((tm, a.shape[1]), lambda i: (i, 0)),
            pl.BlockSpec((tm, b.shape[1]), lambda i: (i, 0)),
            pl.BlockSpec(w.shape, lambda i: (0, 0)),
        ],
        out_specs=pl.BlockSpec((tm, d), lambda i: (i, 0)),
        compiler_params=_params("parallel"),
        name="ab_outproj",
    )(h, a, b, w)


def _mlp_kernel(h_ref, g_ref, wu_ref, wd_ref, o_ref, xn_ref):
    @pl.when(pl.program_id(1) == 0)
    def _():
        h = h_ref[...]
        xn_ref[...] = _rms(h, g_ref[...]).astype(BF16)
        o_ref[...] = h

    u = jnp.maximum(_dot(xn_ref[...], wu_ref[...]), 0.0)
    o_ref[...] += _dot((u * u).astype(BF16), wd_ref[...])


def _mlp(h, g, wu, wd, layer, *, tm, tf):
    t, d = h.shape
    f = wu.shape[2]
    return pl.pallas_call(
        _mlp_kernel,
        out_shape=jax.ShapeDtypeStruct((t, d), F32),
        grid=(t // tm, f // tf),
        in_specs=[
            pl.BlockSpec((tm, d), lambda i, j: (i, 0)),
            pl.BlockSpec((None, 1, d), lambda i, j: (layer, 0, 0)),
            pl.BlockSpec((None, d, tf), lambda i, j: (layer, 0, j)),
            pl.BlockSpec((None, tf, d), lambda i, j: (layer, j, 0)),
        ],
        out_specs=pl.BlockSpec((tm, d), lambda i, j: (i, 0)),
        scratch_shapes=[pltpu.VMEM((tm, d), BF16)],
        compiler_params=_params("parallel", "arbitrary"),
        name="mlp",
    )(h, g, wu, wd)


def _ple_kernel(h_ref, p_ref, wg_ref, wp_ref, fn_ref, o_ref, *, final_norm):
    h = h_ref[...]
    gate = _dot(h.astype(BF16), wg_ref[...])
    emb = _dot(p_ref[...].astype(BF16), wp_ref[...])
    out = h + emb * (1.0 / (1.0 + jnp.exp(-gate)))
    if final_norm:
        out = _rms(out, fn_ref[...])
    o_ref[...] = out


def _ple(h, p, wg, wp, fn, layer, *, tm, final_norm):
    t, d = h.shape
    e = p.shape[2]
    kernel = functools.partial(_ple_kernel, final_norm=final_norm)
    return pl.pallas_call(
        kernel,
        out_shape=jax.ShapeDtypeStruct((t, d), F32),
        grid=(t // tm,),
        in_specs=[
            pl.BlockSpec((tm, d), lambda i: (i, 0)),
            pl.BlockSpec((None, tm, e), lambda i: (layer, i, 0)),
            pl.BlockSpec((None, d, d), lambda i: (layer, 0, 0), pipeline_mode=pl.Buffered(1)),
            pl.BlockSpec((None, e, d), lambda i: (layer, 0, 0), pipeline_mode=pl.Buffered(1)),
            pl.BlockSpec((1, d), lambda i: (0, 0)),
        ],
        out_specs=pl.BlockSpec((tm, d), lambda i: (i, 0)),
        compiler_params=_params("parallel"),
        name="ple",
    )(h, p, wg, wp, fn)


def _ssd_kernel(xbc_ref, par_ref, dt_ref, sel_ref, y_ref, st_ref, pad_ref, *, nchunks, heads):
    c = SSD_CHUNK
    ts = xbc_ref.shape[0]
    gw = heads * SSD_HEAD_DIM
    n = SSD_STATE
    first = pl.program_id(2) == 0

    @pl.when(first)
    def _():
        st_ref[...] = jnp.zeros_like(st_ref)
        pad_ref[0:CONV_HALO, :] = jnp.zeros((CONV_HALO, pad_ref.shape[1]), F32)

    pad_ref[CONV_HALO:CONV_HALO + ts, :] = xbc_ref[...].astype(F32)
    conv = par_ref[0, 0:1, :] + par_ref[0, SSD_CONV:SSD_CONV + 1, :] * pad_ref[CONV_HALO:CONV_HALO + ts, :]
    for back in range(1, SSD_CONV):
        tap = SSD_CONV - back
        conv = conv + par_ref[0, tap:tap + 1, :] * pad_ref[CONV_HALO - back:CONV_HALO - back + ts, :]
    pad_ref[0:CONV_HALO, :] = pad_ref[ts:ts + CONV_HALO, :]
    xbc = _silu(conv)
    xs_all = xbc[:, :gw]
    bm_all = xbc[:, gw:gw + n].astype(BF16)
    cm_all = xbc[:, gw + n:].astype(BF16)

    row = lax.broadcasted_iota(jnp.int32, (c, c), 0)
    col = lax.broadcasted_iota(jnp.int32, (c, c), 1)
    tril = col <= row
    incl = jnp.where(tril, 1.0, 0.0).astype(BF16)
    lane = lax.broadcasted_iota(jnp.int32, (1, LANES), 1)
    lo_half = lane < SSD_HEAD_DIM
    sel = sel_ref[0]
    dsk = par_ref[0, 5:6, 0:gw]
    dt_bias = par_ref[0, 6:7, 0:LANES]
    a2 = -jnp.exp(par_ref[0, 7:8, 0:LANES]) * LOG2E
    st = st_ref[...]

    for ci in range(nchunks):
        sl = slice(ci * c, (ci + 1) * c)
        xs = xs_all[sl, :]
        bm = bm_all[sl, :]
        cm = cm_all[sl, :]
        z = _dot_split_lhs(dt_ref[sl, :], sel, 2) + dt_bias
        dt = jnp.maximum(z, 0.0) + _softplus_neg_abs(z)
        a_cs = _dot_exact_rhs(incl, dt * a2)
        a_cs_t = a_cs.T
        a_last = a_cs[c - 1:c, :]
        cb = _dot_nt(cm, bm)
        y_pairs, ea_pairs, xds_pairs = [], [], []
        for pr in range(heads // 2):
            cols, ms = [], []
            for j in (2 * pr, 2 * pr + 1):
                a_col = jnp.broadcast_to(a_cs[:, j:j + 1], (c, LANES))
                decay = jnp.where(tril, jnp.exp2(a_col - a_cs_t[j:j + 1, :]), 0.0)
                ms.append((cb * decay).astype(BF16))
                cols.append((a_col, jnp.broadcast_to(dt[:, j:j + 1], (c, LANES)), a_last[:, j:j + 1]))
            pick = lambda f: jnp.where(lo_half, f(cols[0]), f(cols[1]))
            xp = xs[:, pr * LANES:(pr + 1) * LANES] * pick(lambda v: v[1])
            ea_pairs.append(pick(lambda v: jnp.exp2(v[0])))
            xds_pairs.append(xp * pick(lambda v: jnp.exp2(v[2] - v[0])))
            lhs = jnp.concatenate(ms, axis=1)
            rhs = jnp.concatenate([jnp.where(lo_half, xp, 0.0), jnp.where(lo_half, 0.0, xp)],
                                  axis=0).astype(BF16)
            y_pairs.append(_dot(lhs, rhs))
        y_diag = jnp.concatenate(y_pairs, axis=1)
        ea_x = jnp.concatenate(ea_pairs, axis=1)
        y_off = _dot(cm, st.astype(BF16)) * ea_x
        st = st * ea_x[c - 1:c, :] + _dot_tn(bm, jnp.concatenate(xds_pairs, axis=1).astype(BF16))
        y_ref[sl, :] = (y_diag + y_off + dsk * xs).astype(BF16)
    st_ref[...] = st


def _ssd(proj, dt_raw, params, *, batch, seq, ts, inner):
    t = proj.shape[0]
    nt = seq // ts
    g = SSD_GROUPS
    gw = inner // g
    heads = gw // SSD_HEAD_DIM
    n = SSD_STATE
    cw = gw + 2 * n

    sel = np.zeros((g, LANES, LANES), np.float32)
    for j in range(heads):
        for gi in range(g):
            sel[gi, gi * heads + j, j] = 1.0
    sel = jnp.asarray(sel, BF16)

    kernel = functools.partial(_ssd_kernel, nchunks=ts // SSD_CHUNK, heads=heads)
    row = lambda b, gi, i: b * nt + i
    return pl.pallas_call(
        kernel,
        out_shape=jax.ShapeDtypeStruct((t, inner), BF16),
        grid=(batch, g, nt),
        in_specs=[
            pl.BlockSpec((ts, cw), lambda b, gi, i: (row(b, gi, i), gi)),
            pl.BlockSpec((1, 8, cw), lambda b, gi, i: (gi, 0, 0)),
            pl.BlockSpec((ts, LANES), lambda b, gi, i: (row(b, gi, i), 0)),
            pl.BlockSpec((1, LANES, LANES), lambda b, gi, i: (gi, 0, 0)),
        ],
        out_specs=pl.BlockSpec((ts, gw), lambda b, gi, i: (row(b, gi, i), gi)),
        scratch_shapes=[
            pltpu.VMEM((n, gw), F32),
            pltpu.VMEM((ts + CONV_HALO, cw), F32),
        ],
        compiler_params=_params("parallel", "parallel", "arbitrary"),
        name="ssd",
    )(proj, params, dt_raw, sel)


def _ssd_outproj_kernel(h_ref, y_ref, zlo_ref, zhi_ref, g_ref, w_ref, o_ref, *, kc):
    inner = y_ref.shape[1]
    half = zlo_ref.shape[1]
    ssq = None
    acc = None
    for k0 in range(0, inner, kc):
        z_ref, zoff = (zlo_ref, k0) if k0 < half else (zhi_ref, k0 - half)
        z = z_ref[:, zoff:zoff + kc].astype(F32)
        yg = y_ref[:, k0:k0 + kc].astype(F32) * _silu(z)
        part = jnp.sum(yg * yg, axis=-1, keepdims=True)
        ssq = part if ssq is None else ssq + part
        term = _dot((yg * g_ref[:, k0:k0 + kc]).astype(BF16), w_ref[k0:k0 + kc, :])
        acc = term if acc is None else acc + term
    o_ref[...] = h_ref[...] + acc * lax.rsqrt(ssq * (1.0 / inner) + EPS)


def _ssd_outproj(h, y, proj, g, w, *, tm, kc):
    t, d = h.shape
    inner = y.shape[1]
    half = inner // 2
    z0 = (proj.shape[1] - inner) // half
    kernel = functools.partial(_ssd_outproj_kernel, kc=kc)
    return pl.pallas_call(
        kernel,
        out_shape=jax.ShapeDtypeStruct((t, d), F32),
        grid=(t // tm,),
        in_specs=[
            pl.BlockSpec((tm, d), lambda i: (i, 0)),
            pl.BlockSpec((tm, inner), lambda i: (i, 0)),
            pl.BlockSpec((tm, half), lambda i: (i, z0)),
            pl.BlockSpec((tm, half), lambda i: (i, z0 + 1)),
            pl.BlockSpec((1, inner), lambda i: (0, 0)),
            pl.BlockSpec((inner, d), lambda i: (0, 0), pipeline_mode=pl.Buffered(1)),
        ],
        out_specs=pl.BlockSpec((tm, d), lambda i: (i, 0)),
        compiler_params=_params("parallel"),
        name="ssd_outproj",
    )(h, y, proj, proj, g, w)


def _pad_cols(w, n):
    return jnp.pad(w, ((0, 0), (0, n - w.shape[1])))


def _group_cols(a, inner):
    g, n = SSD_GROUPS, SSD_STATE
    gw = inner // g
    b0, c0 = inner, inner + g * n
    out = []
    for gi in range(g):
        out += [a[:, gi * gw:(gi + 1) * gw], a[:, b0 + gi * n:b0 + (gi + 1) * n],
                a[:, c0 + gi * n:c0 + (gi + 1) * n]]
    return out


def _ssd_params(conv_w, conv_b, d_skip, dt_bias, a_log, inner):
    g = SSD_GROUPS
    cw = inner // g + 2 * SSD_STATE
    grouped = jnp.concatenate(_group_cols(jnp.concatenate([conv_b[None, :], conv_w], axis=0), inner),
                              axis=1)
    grouped = grouped.reshape(1 + SSD_CONV, g, cw).transpose(1, 0, 2)
    per_head = lambda v, rep: _pad_cols(jnp.repeat(v, rep).reshape(g, -1), cw)[:, None, :]
    return jnp.concatenate([grouped, per_head(d_skip, SSD_HEAD_DIM), per_head(dt_bias, 1),
                            per_head(a_log, 1)], axis=1).astype(F32)


def kernel(x, p, norm_mix, norm_mlp, ab_w_in, ab_w_gate_up, ab_b_gate, ab_gla_norm, ab_w_out,
           ssd_w_in, ssd_conv_w, ssd_conv_b, ssd_dt_bias, ssd_a_log, ssd_d, ssd_norm, ssd_w_out,
           mlp_w_up, mlp_w_down, ple_w_proj, ple_w_gate, final_norm):
    batch, seq, d = x.shape
    t = batch * seq
    depth = p.shape[0]
    tiles = _tiles(t, seq)
    h = x.reshape(t, d)
    row = lambda v: v.reshape(1, -1).astype(F32)
    mlp_norm = norm_mlp.reshape(depth, 1, d).astype(F32)
    mlp_up, mlp_down = mlp_w_up.astype(BF16), mlp_w_down.astype(BF16)
    ple_gate, ple_proj = ple_w_gate.astype(BF16), ple_w_proj.astype(BF16)
    p_rows = p.reshape(depth, t, -1)

    for layer in range(depth):
        j = layer // 2
        if layer % 2 == 0:
            w = ab_w_in[j].astype(BF16)
            glr0 = 3 * SB_HEADS * SB_DIM + 2 * GLA_HEADS * GLA_DK + GLA_HEADS * GLA_DV
            w_main = jnp.concatenate([w[:, :glr0], w[:, glr0 + GLA_RANK:]], axis=1)
            w_glr = _pad_cols(w[:, glr0:glr0 + GLA_RANK], LANES)
            w_gu = jnp.pad(ab_w_gate_up[j], ((0, LANES - GLA_RANK), (0, 0))).astype(BF16)
            proj, log_a = _ab_inproj(h, row(norm_mix[layer]), w_main, w_glr, w_gu,
                                     row(ab_b_gate[j]), tm=tiles.proj_rows, tn=tiles.proj_cols)
            o_sb = _sb_attention(proj, batch=batch, seq=seq, bq=tiles.sb_queries, bk=tiles.sb_keys)
            o_gla = _gla(proj, log_a, row(ab_gla_norm[j]), batch=batch, seq=seq, ts=tiles.gla_rows)
            h = _ab_outproj(h, o_sb, o_gla, ab_w_out[j].astype(BF16), tm=tiles.out_rows)
        else:
            w = ssd_w_in[j].astype(BF16)
            inner = ssd_w_out.shape[1]
            n_zx = w.shape[1] - inner // SSD_HEAD_DIM
            w_perm = jnp.concatenate(_group_cols(w[:, inner:n_zx], inner) + [w[:, :inner]], axis=1)
            proj, dt_raw = _ssd_inproj(h, row(norm_mix[layer]), w_perm, _pad_cols(w[:, n_zx:], LANES),
                                       tm=tiles.proj_rows, tn=tiles.proj_cols)
            params = _ssd_params(ssd_conv_w[j], ssd_conv_b[j], ssd_d[j], ssd_dt_bias[j],
                                 ssd_a_log[j], inner)
            y = _ssd(proj, dt_raw, params, batch=batch, seq=seq, ts=tiles.ssd_rows, inner=inner)
            h = _ssd_outproj(h, y, proj, row(ssd_norm[j]), ssd_w_out[j].astype(BF16),
                             tm=tiles.ssd_out_rows, kc=tiles.ssd_out_k)
        h = _mlp(h, mlp_norm, mlp_up, mlp_down, layer, tm=tiles.mlp_rows, tf=tiles.mlp_hidden)
        h = _ple(h, p_rows, ple_gate, ple_proj, row(final_norm), layer, tm=tiles.ple_rows,
                 final_norm=layer == depth - 1)
    return h.reshape(batch, seq, d)
```

```python
import functools
from typing import NamedTuple

import numpy as np
import jax
import jax.numpy as jnp
from jax import lax
from jax.experimental import pallas as pl
from jax.experimental.pallas import tpu as pltpu

F32 = jnp.float32
BF16 = jnp.bfloat16

EPS = 1e-6
LOG2E = 1.4426950408889634
LANES = 128
VMEM_LIMIT = 56 * 1024 * 1024

SB_HEADS = 8
SB_DIM = 128
GLA_HEADS = 8
GLA_DK = 64
GLA_DV = 128
GLA_RANK = 16
GLA_TAU = 16.0
GLA_CHUNK = 64
SSD_HEAD_DIM = 64
SSD_GROUPS = 8
SSD_STATE = 128
SSD_CONV = 4
SSD_CHUNK = 128
CONV_HALO = 8


def _params(*sem):
    return pltpu.CompilerParams(dimension_semantics=sem, vmem_limit_bytes=VMEM_LIMIT)


class _Tiles(NamedTuple):
    proj_rows: int
    proj_cols: int
    out_rows: int
    mlp_rows: int
    mlp_hidden: int
    ple_rows: int
    ssd_out_rows: int
    ssd_out_k: int
    sb_queries: int
    sb_keys: int
    gla_rows: int
    ssd_rows: int


def _tiles(tokens, seq):
    rows = lambda cap: min(cap, tokens)
    steps = lambda cap: min(cap, seq)
    tiles = _Tiles(proj_rows=rows(1024), proj_cols=1024, out_rows=rows(512), mlp_rows=rows(1024),
                   mlp_hidden=512, ple_rows=rows(512), ssd_out_rows=rows(256), ssd_out_k=256,
                   sb_queries=steps(512), sb_keys=128, gla_rows=steps(1024), ssd_rows=steps(1024))
    assert all(tokens % r == 0 for r in (tiles.proj_rows, tiles.out_rows, tiles.mlp_rows,
                                         tiles.ple_rows, tiles.ssd_out_rows))
    assert seq % tiles.sb_queries == 0 and tiles.sb_queries % tiles.sb_keys == 0
    assert seq % tiles.gla_rows == 0 and tiles.gla_rows % GLA_CHUNK == 0
    assert seq % tiles.ssd_rows == 0 and tiles.ssd_rows % SSD_CHUNK == 0
    return tiles


def _dot(a, b):
    return jnp.dot(a, b, preferred_element_type=F32)


def _dot_nt(a, b):
    return lax.dot_general(a, b, (((1,), (1,)), ((), ())), preferred_element_type=F32)


def _dot_tn(a, b):
    return lax.dot_general(a, b, (((0,), (0,)), ((), ())), preferred_element_type=F32)


def _rms(x, g):
    ms = jnp.mean(x * x, axis=-1, keepdims=True)
    return x * lax.rsqrt(ms + EPS) * g


def _softplus_neg_abs(z):
    return jnp.log(1.0 + jnp.exp(-jnp.abs(z)))


def _log_sigmoid(z):
    return jnp.minimum(z, 0.0) - _softplus_neg_abs(z)


def _silu(z):
    return z * (1.0 / (1.0 + jnp.exp(-z)))


def _split3(x):
    hi = x.astype(BF16)
    r = x - hi.astype(F32)
    mid = r.astype(BF16)
    lo = (r - mid.astype(F32)).astype(BF16)
    return hi, mid, lo


def _dot_exact_rhs(m01, x):
    return _dot(jnp.concatenate([m01] * 3, axis=1), jnp.concatenate(_split3(x), axis=0))


def _dot_split_lhs(x, m01, pieces):
    parts = []
    for _ in range(pieces):
        part = x.astype(BF16)
        x = x - part.astype(F32)
        parts.append(part)
    return _dot(jnp.concatenate(parts, axis=1), jnp.concatenate([m01] * pieces, axis=0))


def _ab_inproj_kernel(x_ref, g_ref, w_ref, wglr_ref, wgu_ref, bg_ref, proj_ref, loga_ref, xn_ref):
    @pl.when(pl.program_id(1) == 0)
    def _():
        xb = _rms(x_ref[...], g_ref[...]).astype(BF16)
        xn_ref[...] = xb
        glr = _dot(xb, wglr_ref[...])
        ga = _dot(glr.astype(BF16), wgu_ref[...]) + bg_ref[...]
        loga_ref[...] = _log_sigmoid(ga) * (1.0 / GLA_TAU)

    proj_ref[...] = _dot(xn_ref[...], w_ref[...]).astype(BF16)


def _ab_inproj(x, g, w, wglr, wgu, bg, *, tm, tn):
    t, d = x.shape
    n = w.shape[1]
    gk = wgu.shape[1]
    return pl.pallas_call(
        _ab_inproj_kernel,
        out_shape=(jax.ShapeDtypeStruct((t, n), BF16), jax.ShapeDtypeStruct((t, gk), F32)),
        grid=(t // tm, n // tn),
        in_specs=[
            pl.BlockSpec((tm, d), lambda i, j: (i, 0)),
            pl.BlockSpec((1, d), lambda i, j: (0, 0)),
            pl.BlockSpec((d, tn), lambda i, j: (0, j)),
            pl.BlockSpec(wglr.shape, lambda i, j: (0, 0)),
            pl.BlockSpec(wgu.shape, lambda i, j: (0, 0)),
            pl.BlockSpec((1, gk), lambda i, j: (0, 0)),
        ],
        out_specs=(pl.BlockSpec((tm, tn), lambda i, j: (i, j)),
                   pl.BlockSpec((tm, gk), lambda i, j: (i, 0))),
        scratch_shapes=[pltpu.VMEM((tm, d), BF16)],
        compiler_params=_params("parallel", "arbitrary"),
        name="ab_inproj",
    )(x, g, w, wglr, wgu, bg)


def _ssd_inproj_kernel(x_ref, g_ref, w_ref, wdt_ref, proj_ref, dt_ref, xn_ref):
    @pl.when(pl.program_id(1) == 0)
    def _():
        xb = _rms(x_ref[...], g_ref[...]).astype(BF16)
        xn_ref[...] = xb
        dt_ref[...] = _dot(xb, wdt_ref[...])

    proj_ref[...] = _dot(xn_ref[...], w_ref[...]).astype(BF16)


def _ssd_inproj(x, g, w, wdt, *, tm, tn):
    t, d = x.shape
    n = w.shape[1]
    nd = wdt.shape[1]
    return pl.pallas_call(
        _ssd_inproj_kernel,
        out_shape=(jax.ShapeDtypeStruct((t, n), BF16), jax.ShapeDtypeStruct((t, nd), F32)),
        grid=(t // tm, n // tn),
        in_specs=[
            pl.BlockSpec((tm, d), lambda i, j: (i, 0)),
            pl.BlockSpec((1, d), lambda i, j: (0, 0)),
            pl.BlockSpec((d, tn), lambda i, j: (0, j)),
            pl.BlockSpec((d, nd), lambda i, j: (0, 0)),
        ],
        out_specs=(pl.BlockSpec((tm, tn), lambda i, j: (i, j)),
                   pl.BlockSpec((tm, nd), lambda i, j: (i, 0))),
        scratch_shapes=[pltpu.VMEM((tm, d), BF16)],
        compiler_params=_params("parallel", "arbitrary"),
        name="ssd_inproj",
    )(x, g, w, wdt)


def _sb_kernel(q_ref, k_ref, v_ref, o_ref, z0_ref, z1_ref, w0_ref, w1_ref, acc_ref, carry_ref,
               *, bq, bk, scale):
    i = pl.program_id(2)
    nd = bq // bk
    q = q_ref[...]
    key = lax.broadcasted_iota(jnp.int32, (bk, bq), 0)
    qry = lax.broadcasted_iota(jnp.int32, (bk, bq), 1)
    col = lax.broadcasted_iota(jnp.int32, (bk, 2 * bk), 1)
    row = lax.broadcasted_iota(jnp.int32, (bk, 2 * bk), 0)
    later2 = jnp.where(jnp.where(col >= bk, col - bk, col) > row, 1.0, 0.0).astype(BF16)

    def scores(start):
        return _dot_nt(k_ref[pl.ds(pl.multiple_of(start, bq), bq), :], q)

    def weighted_values(start, w):
        return _dot_tn(v_ref[pl.ds(pl.multiple_of(start, bq), bq), :], w)

    def block(z, carry, diagonal):
        ws = [None] * nd
        for r in reversed(range(nd)):
            z2 = z[r * bk:(r + 1) * bk, :] * (scale * LOG2E)
            pos = jnp.minimum(z2, 0.0)
            neg = pos - z2
            sp = jnp.log2(1.0 + jnp.exp2(neg + pos))
            log_beta = pos - sp
            l1 = neg - sp
            if diagonal:
                causal = key + r * bk < qry
                l1 = jnp.where(causal, l1, 0.0)
            hi = l1.astype(BF16)
            lo = (l1 - hi.astype(F32)).astype(BF16)
            within = _dot(later2, jnp.concatenate([hi, lo], axis=0))
            w = jnp.exp2(log_beta + (within + carry))
            if diagonal:
                w = jnp.where(causal, w, 0.0)
            ws[r] = w.astype(BF16)
            carry = carry + (within[0:1, :] + l1[0:1, :])
        return carry, jnp.concatenate(ws, axis=0)

    def step(m, z_cur, z_next, w_prev, w_cur):
        start = (i - 1 - m) * bq
        z_next[...] = scores(jnp.maximum(start - bq, 0))
        acc_ref[...] += weighted_values(start + bq, w_prev[...])
        carry, w = block(z_cur[...], carry_ref[...], False)
        w_cur[...] = w
        carry_ref[...] = carry

    carry, w = block(scores(i * bq), jnp.zeros((1, bq), F32), True)
    carry_ref[...] = carry
    w1_ref[...] = w
    acc_ref[...] = jnp.zeros_like(acc_ref)
    z0_ref[...] = scores(jnp.maximum(i - 1, 0) * bq)

    def two_steps(mm, _):
        step(2 * mm, z0_ref, z1_ref, w1_ref, w0_ref)
        step(2 * mm + 1, z1_ref, z0_ref, w0_ref, w1_ref)
        return 0

    lax.fori_loop(0, i // 2, two_steps, 0)
    odd = i % 2 == 1

    @pl.when(odd)
    def _():
        step(i - 1, z0_ref, z1_ref, w1_ref, w0_ref)
        acc_ref[...] += weighted_values(0, w0_ref[...])

    @pl.when(jnp.logical_not(odd))
    def _():
        acc_ref[...] += weighted_values(0, w1_ref[...])

    o_ref[...] = acc_ref[...].T.astype(BF16)


def _sb_attention(proj, *, batch, seq, bq, bk):
    t = proj.shape[0]
    nq = seq // bq
    kernel = functools.partial(_sb_kernel, bq=bq, bk=bk, scale=SB_DIM ** -0.5)
    return pl.pallas_call(
        kernel,
        out_shape=jax.ShapeDtypeStruct((t, SB_HEADS * SB_DIM), BF16),
        grid=(batch, SB_HEADS, nq),
        in_specs=[
            pl.BlockSpec((bq, SB_DIM), lambda b, h, i: (b * nq + i, h)),
            pl.BlockSpec((seq, SB_DIM), lambda b, h, i: (b, SB_HEADS + h)),
            pl.BlockSpec((seq, SB_DIM), lambda b, h, i: (b, 2 * SB_HEADS + h)),
        ],
        out_specs=pl.BlockSpec((bq, SB_DIM), lambda b, h, i: (b * nq + i, h)),
        scratch_shapes=[
            pltpu.VMEM((bq, bq), F32), pltpu.VMEM((bq, bq), F32),
            pltpu.VMEM((bq, bq), BF16), pltpu.VMEM((bq, bq), BF16),
            pltpu.VMEM((SB_DIM, bq), F32),
            pltpu.VMEM((1, bq), F32),
        ],
        compiler_params=_params("parallel", "parallel", "arbitrary"),
        name="sb_attention",
    )(proj, proj, proj)


def _gla_kernel(q_ref, k_ref, v_ref, go_ref, la_ref, gn_ref, o_ref, st_ref, *, nchunks):
    c = GLA_CHUNK

    @pl.when(pl.program_id(2) == 0)
    def _():
        st_ref[...] = jnp.zeros_like(st_ref)

    kw, vw = 2 * GLA_DK, 2 * GLA_DV
    row = lax.broadcasted_iota(jnp.int32, (c, 3 * c), 0)
    col = lax.broadcasted_iota(jnp.int32, (c, 3 * c), 1)
    incl3 = jnp.where(col % c <= row, 1.0, 0.0).astype(BF16)
    srow = lax.broadcasted_iota(jnp.int32, (c, kw), 0)
    scol = lax.broadcasted_iota(jnp.int32, (c, kw), 1)
    causal2 = scol % c <= srow
    krow = lax.broadcasted_iota(jnp.int32, (kw, kw), 0)
    kcol = lax.broadcasted_iota(jnp.int32, (kw, kw), 1)
    key_head = (krow < c) == (kcol < GLA_DK)
    vrow = lax.broadcasted_iota(jnp.int32, (kw, vw), 0)
    vcol = lax.broadcasted_iota(jnp.int32, (kw, vw), 1)
    val_head = (vrow < c) == (vcol < GLA_DV)
    trow = lax.broadcasted_iota(jnp.int32, (vw, kw), 0)
    tcol = lax.broadcasted_iota(jnp.int32, (vw, kw), 1)
    same_head = (trow < GLA_DV) == (tcol < GLA_DK)
    gn = gn_ref[...]
    st = st_ref[...]

    for n in range(nchunks):
        sl = pl.ds(n * c, c)
        gcum = _dot(incl3, jnp.concatenate(_split3(la_ref[sl, :]), axis=0))
        glast = gcum[c - 1:c, :]
        q = q_ref[sl, :].astype(F32) * (GLA_DK ** -0.5)
        k = k_ref[sl, :].astype(F32)
        v = v_ref[sl, :]
        q_dec = (q * jnp.exp(gcum)).astype(BF16)
        k_inv = (k * jnp.exp(-gcum)).astype(BF16)
        k_end = (k * jnp.exp(glast - gcum)).astype(BF16)
        k2 = jnp.where(key_head, jnp.concatenate([k_inv, k_inv], axis=0), jnp.zeros((kw, kw), BF16))
        scores = jnp.where(causal2, _dot_nt(q_dec, k2), 0.0)
        v2 = jnp.where(val_head, jnp.concatenate([v, v], axis=0), jnp.zeros((kw, vw), BF16))
        o = _dot(scores.astype(BF16), v2) + _dot_nt(q_dec, st.astype(BF16))
        st = st * jnp.exp(glast) + jnp.where(same_head, _dot_tn(v, k_end), 0.0)
        gate = _silu(go_ref[sl, :].astype(F32))
        outs = []
        for a in range(2):
            hs = slice(a * GLA_DV, (a + 1) * GLA_DV)
            outs.append(_rms(o[:, hs], gn) * gate[:, hs])
        o_ref[sl, :] = jnp.concatenate(outs, axis=1).astype(BF16)
    st_ref[...] = st


def _gla(proj, log_a, gnorm, *, batch, seq, ts):
    t = proj.shape[0]
    nt = seq // ts
    pairs = GLA_HEADS // 2
    kw, vw = 2 * GLA_DK, 2 * GLA_DV
    q0 = 3 * SB_HEADS * SB_DIM // kw
    k0 = q0 + GLA_HEADS * GLA_DK // kw
    v0 = (3 * SB_HEADS * SB_DIM + 2 * GLA_HEADS * GLA_DK) // vw
    g0 = v0 + GLA_HEADS * GLA_DV // vw
    kernel = functools.partial(_gla_kernel, nchunks=ts // GLA_CHUNK)
    return pl.pallas_call(
        kernel,
        out_shape=jax.ShapeDtypeStruct((t, GLA_HEADS * GLA_DV), BF16),
        grid=(batch, pairs, nt),
        in_specs=[
            pl.BlockSpec((ts, kw), lambda b, p, i: (b * nt + i, q0 + p)),
            pl.BlockSpec((ts, kw), lambda b, p, i: (b * nt + i, k0 + p)),
            pl.BlockSpec((ts, vw), lambda b, p, i: (b * nt + i, v0 + p)),
            pl.BlockSpec((ts, vw), lambda b, p, i: (b * nt + i, g0 + p)),
            pl.BlockSpec((ts, kw), lambda b, p, i: (b * nt + i, p)),
            pl.BlockSpec((1, GLA_DV), lambda b, p, i: (0, 0)),
        ],
        out_specs=pl.BlockSpec((ts, vw), lambda b, p, i: (b * nt + i, p)),
        scratch_shapes=[pltpu.VMEM((vw, kw), F32)],
        compiler_params=_params("parallel", "parallel", "arbitrary"),
        name="gla",
    )(proj, proj, proj, proj, log_a, gnorm)


def _ab_outproj_kernel(h_ref, a_ref, b_ref, w_ref, o_ref):
    ka = a_ref.shape[1]
    acc = _dot(a_ref[...], w_ref[0:ka, :]) + _dot(b_ref[...], w_ref[ka:, :])
    o_ref[...] = h_ref[...] + acc


def _ab_outproj(h, a, b, w, *, tm):
    t, d = h.shape
    return pl.pallas_call(
        _ab_outproj_kernel,
        out_shape=jax.ShapeDtypeStruct((t, d), F32),
        grid=(t // tm,),
        in_specs=[
            pl.BlockSpec((tm, d), lambda i: (i, 0)),
            pl.BlockSpec((tm, a.shape[1]), lambda i: (i, 0)),
            pl.BlockSpec((tm, b.shape[1]), lambda i: (i, 0)),
            pl.BlockSpec(w.shape, lambda i: (0, 0)),
        ],
        out_specs=pl.BlockSpec((tm, d), lambda i: (i, 0)),
        compiler_params=_params("parallel"),
        name="ab_outproj",
    )(h, a, b, w)


def _mlp_kernel(h_ref, g_ref, wu_ref, wd_ref, o_ref, xn_ref):
    @pl.when(pl.program_id(1) == 0)
    def _():
        h = h_ref[...]
        xn_ref[...] = _rms(h, g_ref[...]).astype(BF16)
        o_ref[...] = h

    u = jnp.maximum(_dot(xn_ref[...], wu_ref[...]), 0.0)
    o_ref[...] += _dot((u * u).astype(BF16), wd_ref[...])


def _mlp(h, g, wu, wd, layer, *, tm, tf):
    t, d = h.shape
    f = wu.shape[2]
    return pl.pallas_call(
        _mlp_kernel,
        out_shape=jax.ShapeDtypeStruct((t, d), F32),
        grid=(t // tm, f // tf),
        in_specs=[
            pl.BlockSpec((tm, d), lambda i, j: (i, 0)),
            pl.BlockSpec((None, 1, d), lambda i, j: (layer, 0, 0)),
            pl.BlockSpec((None, d, tf), lambda i, j: (layer, 0, j)),
            pl.BlockSpec((None, tf, d), lambda i, j: (layer, j, 0)),
        ],
        out_specs=pl.BlockSpec((tm, d), lambda i, j: (i, 0)),
        scratch_shapes=[pltpu.VMEM((tm, d), BF16)],
        compiler_params=_params("parallel", "arbitrary"),
        name="mlp",
    )(h, g, wu, wd)


def _ple_kernel(h_ref, p_ref, wg_ref, wp_ref, fn_ref, o_ref, *, final_norm):
    h = h_ref[...]
    gate = _dot(h.astype(BF16), wg_ref[...])
    emb = _dot(p_ref[...].astype(BF16), wp_ref[...])
    out = h + emb * (1.0 / (1.0 + jnp.exp(-gate)))
    if final_norm:
        out = _rms(out, fn_ref[...])
    o_ref[...] = out


def _ple(h, p, wg, wp, fn, layer, *, tm, final_norm):
    t, d = h.shape
    e = p.shape[2]
    kernel = functools.partial(_ple_kernel, final_norm=final_norm)
    return pl.pallas_call(
        kernel,
        out_shape=jax.ShapeDtypeStruct((t, d), F32),
        grid=(t // tm,),
        in_specs=[
            pl.BlockSpec((tm, d), lambda i: (i, 0)),
            pl.BlockSpec((None, tm, e), lambda i: (layer, i, 0)),
            pl.BlockSpec((None, d, d), lambda i: (layer, 0, 0), pipeline_mode=pl.Buffered(1)),
            pl.BlockSpec((None, e, d), lambda i: (layer, 0, 0), pipeline_mode=pl.Buffered(1)),
            pl.BlockSpec((1, d), lambda i: (0, 0)),
        ],
        out_specs=pl.BlockSpec((tm, d), lambda i: (i, 0)),
        compiler_params=_params("parallel"),
        name="ple",
    )(h, p, wg, wp, fn)


def _ssd_kernel(xbc_ref, par_ref, dt_ref, sel_ref, y_ref, st_ref, pad_ref, *, nchunks, heads):
    c = SSD_CHUNK
    ts = xbc_ref.shape[0]
    gw = heads * SSD_HEAD_DIM
    n = SSD_STATE
    first = pl.program_id(2) == 0

    @pl.when(first)
    def _():
        st_ref[...] = jnp.zeros_like(st_ref)
        pad_ref[0:CONV_HALO, :] = jnp.zeros((CONV_HALO, pad_ref.shape[1]), F32)

    pad_ref[CONV_HALO:CONV_HALO + ts, :] = xbc_ref[...].astype(F32)
    conv = par_ref[0, 0:1, :] + par_ref[0, SSD_CONV:SSD_CONV + 1, :] * pad_ref[CONV_HALO:CONV_HALO + ts, :]
    for back in range(1, SSD_CONV):
        tap = SSD_CONV - back
        conv = conv + par_ref[0, tap:tap + 1, :] * pad_ref[CONV_HALO - back:CONV_HALO - back + ts, :]
    pad_ref[0:CONV_HALO, :] = pad_ref[ts:ts + CONV_HALO, :]
    xbc = _silu(conv)
    xs_all = xbc[:, :gw]
    bm_all = xbc[:, gw:gw + n].astype(BF16)
    cm_all = xbc[:, gw + n:].astype(BF16)

    row = lax.broadcasted_iota(jnp.int32, (c, c), 0)
    col = lax.broadcasted_iota(jnp.int32, (c, c), 1)
    tril = col <= row
    incl = jnp.where(tril, 1.0, 0.0).astype(BF16)
    lane = lax.broadcasted_iota(jnp.int32, (1, LANES), 1)
    lo_half = lane < SSD_HEAD_DIM
    sel = sel_ref[0]
    dsk = par_ref[0, 5:6, 0:gw]
    dt_bias = par_ref[0, 6:7, 0:LANES]
    a2 = -jnp.exp(par_ref[0, 7:8, 0:LANES]) * LOG2E
    st = st_ref[...]

    for ci in range(nchunks):
        sl = slice(ci * c, (ci + 1) * c)
        xs = xs_all[sl, :]
        bm = bm_all[sl, :]
        cm = cm_all[sl, :]
        z = _dot_split_lhs(dt_ref[sl, :], sel, 2) + dt_bias
        dt = jnp.maximum(z, 0.0) + _softplus_neg_abs(z)
        a_cs = _dot_exact_rhs(incl, dt * a2)
        a_cs_t = a_cs.T
        a_last = a_cs[c - 1:c, :]
        cb = _dot_nt(cm, bm)
        y_pairs, ea_pairs, xds_pairs = [], [], []
        for pr in range(heads // 2):
            cols, ms = [], []
            for j in (2 * pr, 2 * pr + 1):
                a_col = jnp.broadcast_to(a_cs[:, j:j + 1], (c, LANES))
                decay = jnp.where(tril, jnp.exp2(a_col - a_cs_t[j:j + 1, :]), 0.0)
                ms.append((cb * decay).astype(BF16))
                cols.append((a_col, jnp.broadcast_to(dt[:, j:j + 1], (c, LANES)), a_last[:, j:j + 1]))
            pick = lambda f: jnp.where(lo_half, f(cols[0]), f(cols[1]))
            xp = xs[:, pr * LANES:(pr + 1) * LANES] * pick(lambda v: v[1])
            ea_pairs.append(pick(lambda v: jnp.exp2(v[0])))
            xds_pairs.append(xp * pick(lambda v: jnp.exp2(v[2] - v[0])))
            lhs = jnp.concatenate(ms, axis=1)
            rhs = jnp.concatenate([jnp.where(lo_half, xp, 0.0), jnp.where(lo_half, 0.0, xp)],
                                  axis=0).astype(BF16)
            y_pairs.append(_dot(lhs, rhs))
        y_diag = jnp.concatenate(y_pairs, axis=1)
        ea_x = jnp.concatenate(ea_pairs, axis=1)
        y_off = _dot(cm, st.astype(BF16)) * ea_x
        st = st * ea_x[c - 1:c, :] + _dot_tn(bm, jnp.concatenate(xds_pairs, axis=1).astype(BF16))
        y_ref[sl, :] = (y_diag + y_off + dsk * xs).astype(BF16)
    st_ref[...] = st


def _ssd(proj, dt_raw, params, *, batch, seq, ts, inner):
    t = proj.shape[0]
    nt = seq // ts
    g = SSD_GROUPS
    gw = inner // g
    heads = gw // SSD_HEAD_DIM
    n = SSD_STATE
    cw = gw + 2 * n

    sel = np.zeros((g, LANES, LANES), np.float32)
    for j in range(heads):
        for gi in range(g):
            sel[gi, gi * heads + j, j] = 1.0
    sel = jnp.asarray(sel, BF16)

    kernel = functools.partial(_ssd_kernel, nchunks=ts // SSD_CHUNK, heads=heads)
    row = lambda b, gi, i: b * nt + i
    return pl.pallas_call(
        kernel,
        out_shape=jax.ShapeDtypeStruct((t, inner), BF16),
        grid=(batch, g, nt),
        in_specs=[
            pl.BlockSpec((ts, cw), lambda b, gi, i: (row(b, gi, i), gi)),
            pl.BlockSpec((1, 8, cw), lambda b, gi, i: (gi, 0, 0)),
            pl.BlockSpec((ts, LANES), lambda b, gi, i: (row(b, gi, i), 0)),
            pl.BlockSpec((1, LANES, LANES), lambda b, gi, i: (gi, 0, 0)),
        ],
        out_specs=pl.BlockSpec((ts, gw), lambda b, gi, i: (row(b, gi, i), gi)),
        scratch_shapes=[
            pltpu.VMEM((n, gw), F32),
            pltpu.VMEM((ts + CONV_HALO, cw), F32),
        ],
        compiler_params=_params("parallel", "parallel", "arbitrary"),
        name="ssd",
    )(proj, params, dt_raw, sel)


def _ssd_outproj_kernel(h_ref, y_ref, zlo_ref, zhi_ref, g_ref, w_ref, o_ref, *, kc):
    inner = y_ref.shape[1]
    half = zlo_ref.shape[1]
    ssq = None
    acc = None
    for k0 in range(0, inner, kc):
        z_ref, zoff = (zlo_ref, k0) if k0 < half else (zhi_ref, k0 - half)
        z = z_ref[:, zoff:zoff + kc].astype(F32)
        yg = y_ref[:, k0:k0 + kc].astype(F32) * _silu(z)
        part = jnp.sum(yg * yg, axis=-1, keepdims=True)
        ssq = part if ssq is None else ssq + part
        term = _dot((yg * g_ref[:, k0:k0 + kc]).astype(BF16), w_ref[k0:k0 + kc, :])
        acc = term if acc is None else acc + term
    o_ref[...] = h_ref[...] + acc * lax.rsqrt(ssq * (1.0 / inner) + EPS)


def _ssd_outproj(h, y, proj, g, w, *, tm, kc):
    t, d = h.shape
    inner = y.shape[1]
    half = inner // 2
    z0 = (proj.shape[1] - inner) // half
    kernel = functools.partial(_ssd_outproj_kernel, kc=kc)
    return pl.pallas_call(
        kernel,
        out_shape=jax.ShapeDtypeStruct((t, d), F32),
        grid=(t // tm,),
        in_specs=[
            pl.BlockSpec((tm, d), lambda i: (i, 0)),
            pl.BlockSpec((tm, inner), lambda i: (i, 0)),
            pl.BlockSpec((tm, half), lambda i: (i, z0)),
            pl.BlockSpec((tm, half), lambda i: (i, z0 + 1)),
            pl.BlockSpec((1, inner), lambda i: (0, 0)),
            pl.BlockSpec((inner, d), lambda i: (0, 0), pipeline_mode=pl.Buffered(1)),
        ],
        out_specs=pl.BlockSpec((tm, d), lambda i: (i, 0)),
        compiler_params=_params("parallel"),
        name="ssd_outproj",
    )(h, y, proj, proj, g, w)


def _pad_cols(w, n):
    return jnp.pad(w, ((0, 0), (0, n - w.shape[1])))


def _group_cols(a, inner):
    g, n = SSD_GROUPS, SSD_STATE
    gw = inner // g
    b0, c0 = inner, inner + g * n
    out = []
    for gi in range(g):
        out += [a[:, gi * gw:(gi + 1) * gw], a[:, b0 + gi * n:b0 + (gi + 1) * n],
                a[:, c0 + gi * n:c0 + (gi + 1) * n]]
    return out


def _ssd_params(conv_w, conv_b, d_skip, dt_bias, a_log, inner):
    g = SSD_GROUPS
    cw = inner // g + 2 * SSD_STATE
    grouped = jnp.concatenate(_group_cols(jnp.concatenate([conv_b[None, :], conv_w], axis=0), inner),
                              axis=1)
    grouped = grouped.reshape(1 + SSD_CONV, g, cw).transpose(1, 0, 2)
    per_head = lambda v, rep: _pad_cols(jnp.repeat(v, rep).reshape(g, -1), cw)[:, None, :]
    return jnp.concatenate([grouped, per_head(d_skip, SSD_HEAD_DIM), per_head(dt_bias, 1),
                            per_head(a_log, 1)], axis=1).astype(F32)


def kernel(x, p, norm_mix, norm_mlp, ab_w_in, ab_w_gate_up, ab_b_gate, ab_gla_norm, ab_w_out,
           ssd_w_in, ssd_conv_w, ssd_conv_b, ssd_dt_bias, ssd_a_log, ssd_d, ssd_norm, ssd_w_out,
           mlp_w_up, mlp_w_down, ple_w_proj, ple_w_gate, final_norm):
    batch, seq, d = x.shape
    t = batch * seq
    depth = p.shape[0]
    tiles = _tiles(t, seq)
    h = x.reshape(t, d)
    row = lambda v: v.reshape(1, -1).astype(F32)
    mlp_norm = norm_mlp.reshape(depth, 1, d).astype(F32)
    mlp_up, mlp_down = mlp_w_up.astype(BF16), mlp_w_down.astype(BF16)
    ple_gate, ple_proj = ple_w_gate.astype(BF16), ple_w_proj.astype(BF16)
    p_rows = p.reshape(depth, t, -1)

    for layer in range(depth):
        j = layer // 2
        if layer % 2 == 0:
            w = ab_w_in[j].astype(BF16)
            glr0 = 3 * SB_HEADS * SB_DIM + 2 * GLA_HEADS * GLA_DK + GLA_HEADS * GLA_DV
            w_main = jnp.concatenate([w[:, :glr0], w[:, glr0 + GLA_RANK:]], axis=1)
            w_glr = _pad_cols(w[:, glr0:glr0 + GLA_RANK], LANES)
            w_gu = jnp.pad(ab_w_gate_up[j], ((0, LANES - GLA_RANK), (0, 0))).astype(BF16)
            proj, log_a = _ab_inproj(h, row(norm_mix[layer]), w_main, w_glr, w_gu,
                                     row(ab_b_gate[j]), tm=tiles.proj_rows, tn=tiles.proj_cols)
            o_sb = _sb_attention(proj, batch=batch, seq=seq, bq=tiles.sb_queries, bk=tiles.sb_keys)
            o_gla = _gla(proj, log_a, row(ab_gla_norm[j]), batch=batch, seq=seq, ts=tiles.gla_rows)
            h = _ab_outproj(h, o_sb, o_gla, ab_w_out[j].astype(BF16), tm=tiles.out_rows)
        else:
            w = ssd_w_in[j].astype(BF16)
            inner = ssd_w_out.shape[1]
            n_zx = w.shape[1] - inner // SSD_HEAD_DIM
            w_perm = jnp.concatenate(_group_cols(w[:, inner:n_zx], inner) + [w[:, :inner]], axis=1)
            proj, dt_raw = _ssd_inproj(h, row(norm_mix[layer]), w_perm, _pad_cols(w[:, n_zx:], LANES),
                                       tm=tiles.proj_rows, tn=tiles.proj_cols)
            params = _ssd_params(ssd_conv_w[j], ssd_conv_b[j], ssd_d[j], ssd_dt_bias[j],
                                 ssd_a_log[j], inner)
            y = _ssd(proj, dt_raw, params, batch=batch, seq=seq, ts=tiles.ssd_rows, inner=inner)
            h = _ssd_outproj(h, y, proj, row(ssd_norm[j]), ssd_w_out[j].astype(BF16),
                             tm=tiles.ssd_out_rows, kc=tiles.ssd_out_k)
        h = _mlp(h, mlp_norm, mlp_up, mlp_down, layer, tm=tiles.mlp_rows, tf=tiles.mlp_hidden)
        h = _ple(h, p_rows, ple_gate, ple_proj, row(final_norm), layer, tm=tiles.ple_rows,
                 final_norm=layer == depth - 1)
    return h.reshape(batch, seq, d)
```

```python
import functools
from typing import NamedTuple

import numpy as np
import jax
import jax.numpy as jnp
from jax import lax
from jax.experimental import pallas as pl
from jax.experimental.pallas import tpu as pltpu

F32 = jnp.float32
BF16 = jnp.bfloat16

EPS = 1e-6
LOG2E = 1.4426950408889634
LANES = 128
VMEM_LIMIT = 56 * 1024 * 1024

SB_HEADS = 8
SB_DIM = 128
GLA_HEADS = 8
GLA_DK = 64
GLA_DV = 128
GLA_RANK = 16
GLA_TAU = 16.0
GLA_CHUNK = 64
SSD_HEAD_DIM = 64
SSD_GROUPS = 8
SSD_STATE = 128
SSD_CONV = 4
SSD_CHUNK = 128
CONV_HALO = 8


def _params(*sem):
    return pltpu.CompilerParams(dimension_semantics=sem, vmem_limit_bytes=VMEM_LIMIT)


class _Tiles(NamedTuple):
    proj_rows: int
    proj_cols: int
    out_rows: int
    mlp_rows: int
    mlp_hidden: int
    ple_rows: int
    ssd_out_rows: int
    ssd_out_k: int
    sb_queries: int
    sb_keys: int
    gla_rows: int
    ssd_rows: int


def _tiles(tokens, seq):
    rows = lambda cap: min(cap, tokens)
    steps = lambda cap: min(cap, seq)
    tiles = _Tiles(proj_rows=rows(1024), proj_cols=1024, out_rows=rows(512), mlp_rows=rows(1024),
                   mlp_hidden=512, ple_rows=rows(512), ssd_out_rows=rows(256), ssd_out_k=256,
                   sb_queries=steps(512), sb_keys=128, gla_rows=steps(2048), ssd_rows=steps(1024))
    assert all(tokens % r == 0 for r in (tiles.proj_rows, tiles.out_rows, tiles.mlp_rows,
                                         tiles.ple_rows, tiles.ssd_out_rows))
    assert seq % tiles.sb_queries == 0 and tiles.sb_queries % tiles.sb_keys == 0
    assert seq % tiles.gla_rows == 0 and tiles.gla_rows % GLA_CHUNK == 0
    assert seq % tiles.ssd_rows == 0 and tiles.ssd_rows % SSD_CHUNK == 0
    return tiles


def _dot(a, b):
    return jnp.dot(a, b, preferred_element_type=F32)


def _dot_nt(a, b):
    return lax.dot_general(a, b, (((1,), (1,)), ((), ())), preferred_element_type=F32)


def _dot_tn(a, b):
    return lax.dot_general(a, b, (((0,), (0,)), ((), ())), preferred_element_type=F32)


def _rms(x, g):
    ms = jnp.mean(x * x, axis=-1, keepdims=True)
    return x * lax.rsqrt(ms + EPS) * g


def _softplus_neg_abs(z):
    return jnp.log(1.0 + jnp.exp(-jnp.abs(z)))


def _log_sigmoid(z):
    return jnp.minimum(z, 0.0) - _softplus_neg_abs(z)


def _silu(z):
    return z * (1.0 / (1.0 + jnp.exp(-z)))


def _split3(x):
    hi = x.astype(BF16)
    r = x - hi.astype(F32)
    mid = r.astype(BF16)
    lo = (r - mid.astype(F32)).astype(BF16)
    return hi, mid, lo


def _dot_exact_rhs(m01, x):
    return _dot(jnp.concatenate([m01] * 3, axis=1), jnp.concatenate(_split3(x), axis=0))


def _dot_split_lhs(x, m01, pieces):
    parts = []
    for _ in range(pieces):
        part = x.astype(BF16)
        x = x - part.astype(F32)
        parts.append(part)
    return _dot(jnp.concatenate(parts, axis=1), jnp.concatenate([m01] * pieces, axis=0))


def _ab_inproj_kernel(x_ref, g_ref, w_ref, wglr_ref, wgu_ref, bg_ref, proj_ref, loga_ref, xn_ref):
    @pl.when(pl.program_id(1) == 0)
    def _():
        xb = _rms(x_ref[...], g_ref[...]).astype(BF16)
        xn_ref[...] = xb
        glr = _dot(xb, wglr_ref[...])
        ga = _dot(glr.astype(BF16), wgu_ref[...]) + bg_ref[...]
        loga_ref[...] = _log_sigmoid(ga) * (1.0 / GLA_TAU)

    proj_ref[...] = _dot(xn_ref[...], w_ref[...]).astype(BF16)


def _ab_inproj(x, g, w, wglr, wgu, bg, *, tm, tn):
    t, d = x.shape
    n = w.shape[1]
    gk = wgu.shape[1]
    return pl.pallas_call(
        _ab_inproj_kernel,
        out_shape=(jax.ShapeDtypeStruct((t, n), BF16), jax.ShapeDtypeStruct((t, gk), F32)),
        grid=(t // tm, n // tn),
        in_specs=[
            pl.BlockSpec((tm, d), lambda i, j: (i, 0)),
            pl.BlockSpec((1, d), lambda i, j: (0, 0)),
            pl.BlockSpec((d, tn), lambda i, j: (0, j)),
            pl.BlockSpec(wglr.shape, lambda i, j: (0, 0)),
            pl.BlockSpec(wgu.shape, lambda i, j: (0, 0)),
            pl.BlockSpec((1, gk), lambda i, j: (0, 0)),
        ],
        out_specs=(pl.BlockSpec((tm, tn), lambda i, j: (i, j)),
                   pl.BlockSpec((tm, gk), lambda i, j: (i, 0))),
        scratch_shapes=[pltpu.VMEM((tm, d), BF16)],
        compiler_params=_params("parallel", "arbitrary"),
        name="ab_inproj",
    )(x, g, w, wglr, wgu, bg)


def _ssd_inproj_kernel(x_ref, g_ref, w_ref, wdt_ref, proj_ref, dt_ref, xn_ref):
    @pl.when(pl.program_id(1) == 0)
    def _():
        xb = _rms(x_ref[...], g_ref[...]).astype(BF16)
        xn_ref[...] = xb
        dt_ref[...] = _dot(xb, wdt_ref[...])

    proj_ref[...] = _dot(xn_ref[...], w_ref[...]).astype(BF16)


def _ssd_inproj(x, g, w, wdt, *, tm, tn):
    t, d = x.shape
    n = w.shape[1]
    nd = wdt.shape[1]
    return pl.pallas_call(
        _ssd_inproj_kernel,
        out_shape=(jax.ShapeDtypeStruct((t, n), BF16), jax.ShapeDtypeStruct((t, nd), F32)),
        grid=(t // tm, n // tn),
        in_specs=[
            pl.BlockSpec((tm, d), lambda i, j: (i, 0)),
            pl.BlockSpec((1, d), lambda i, j: (0, 0)),
            pl.BlockSpec((d, tn), lambda i, j: (0, j)),
            pl.BlockSpec((d, nd), lambda i, j: (0, 0)),
        ],
        out_specs=(pl.BlockSpec((tm, tn), lambda i, j: (i, j)),
                   pl.BlockSpec((tm, nd), lambda i, j: (i, 0))),
        scratch_shapes=[pltpu.VMEM((tm, d), BF16)],
        compiler_params=_params("parallel", "arbitrary"),
        name="ssd_inproj",
    )(x, g, w, wdt)


def _sb_kernel(q_ref, k_ref, v_ref, o_ref, z0_ref, z1_ref, w0_ref, w1_ref, acc_ref, carry_ref,
               *, bq, bk, scale):
    i = pl.program_id(2)
    nd = bq // bk
    q = q_ref[...]
    key = lax.broadcasted_iota(jnp.int32, (bk, bq), 0)
    qry = lax.broadcasted_iota(jnp.int32, (bk, bq), 1)
    col = lax.broadcasted_iota(jnp.int32, (bk, 2 * bk), 1)
    row = lax.broadcasted_iota(jnp.int32, (bk, 2 * bk), 0)
    later2 = jnp.where(jnp.where(col >= bk, col - bk, col) > row, 1.0, 0.0).astype(BF16)

    def scores(start):
        return _dot_nt(k_ref[pl.ds(pl.multiple_of(start, bq), bq), :], q)

    def weighted_values(start, w):
        return _dot_tn(v_ref[pl.ds(pl.multiple_of(start, bq), bq), :], w)

    def block(z, carry, diagonal):
        ws = [None] * nd
        for r in reversed(range(nd)):
            z2 = z[r * bk:(r + 1) * bk, :] * (scale * LOG2E)
            pos = jnp.minimum(z2, 0.0)
            neg = pos - z2
            sp = jnp.log2(1.0 + jnp.exp2(neg + pos))
            log_beta = pos - sp
            l1 = neg - sp
            if diagonal:
                causal = key + r * bk < qry
                l1 = jnp.where(causal, l1, 0.0)
            hi = l1.astype(BF16)
            lo = (l1 - hi.astype(F32)).astype(BF16)
            within = _dot(later2, jnp.concatenate([hi, lo], axis=0))
            w = jnp.exp2(log_beta + (within + carry))
            if diagonal:
                w = jnp.where(causal, w, 0.0)
            ws[r] = w.astype(BF16)
            carry = carry + (within[0:1, :] + l1[0:1, :])
        return carry, jnp.concatenate(ws, axis=0)

    def step(m, z_cur, z_next, w_prev, w_cur):
        start = (i - 1 - m) * bq
        z_next[...] = scores(jnp.maximum(start - bq, 0))
        acc_ref[...] += weighted_values(start + bq, w_prev[...])
        carry, w = block(z_cur[...], carry_ref[...], False)
        w_cur[...] = w
        carry_ref[...] = carry

    carry, w = block(scores(i * bq), jnp.zeros((1, bq), F32), True)
    carry_ref[...] = carry
    w1_ref[...] = w
    acc_ref[...] = jnp.zeros_like(acc_ref)
    z0_ref[...] = scores(jnp.maximum(i - 1, 0) * bq)

    def two_steps(mm, _):
        step(2 * mm, z0_ref, z1_ref, w1_ref, w0_ref)
        step(2 * mm + 1, z1_ref, z0_ref, w0_ref, w1_ref)
        return 0

    lax.fori_loop(0, i // 2, two_steps, 0)
    odd = i % 2 == 1

    @pl.when(odd)
    def _():
        step(i - 1, z0_ref, z1_ref, w1_ref, w0_ref)
        acc_ref[...] += weighted_values(0, w0_ref[...])

    @pl.when(jnp.logical_not(odd))
    def _():
        acc_ref[...] += weighted_values(0, w1_ref[...])

    o_ref[...] = acc_ref[...].T.astype(BF16)


def _sb_attention(proj, *, batch, seq, bq, bk):
    t = proj.shape[0]
    nq = seq // bq
    kernel = functools.partial(_sb_kernel, bq=bq, bk=bk, scale=SB_DIM ** -0.5)
    return pl.pallas_call(
        kernel,
        out_shape=jax.ShapeDtypeStruct((t, SB_HEADS * SB_DIM), BF16),
        grid=(batch, SB_HEADS, nq),
        in_specs=[
            pl.BlockSpec((bq, SB_DIM), lambda b, h, i: (b * nq + i, h)),
            pl.BlockSpec((seq, SB_DIM), lambda b, h, i: (b, SB_HEADS + h)),
            pl.BlockSpec((seq, SB_DIM), lambda b, h, i: (b, 2 * SB_HEADS + h)),
        ],
        out_specs=pl.BlockSpec((bq, SB_DIM), lambda b, h, i: (b * nq + i, h)),
        scratch_shapes=[
            pltpu.VMEM((bq, bq), F32), pltpu.VMEM((bq, bq), F32),
            pltpu.VMEM((bq, bq), BF16), pltpu.VMEM((bq, bq), BF16),
            pltpu.VMEM((SB_DIM, bq), F32),
            pltpu.VMEM((1, bq), F32),
        ],
        compiler_params=_params("parallel", "parallel", "arbitrary"),
        name="sb_attention",
    )(proj, proj, proj)


def _gla_kernel(q_ref, k_ref, v_ref, go_ref, la_ref, gn_ref, o_ref, st_ref, *, nchunks):
    c = GLA_CHUNK

    @pl.when(pl.program_id(2) == 0)
    def _():
        st_ref[...] = jnp.zeros_like(st_ref)

    kw, vw = 2 * GLA_DK, 2 * GLA_DV
    row = lax.broadcasted_iota(jnp.int32, (c, 3 * c), 0)
    col = lax.broadcasted_iota(jnp.int32, (c, 3 * c), 1)
    incl3 = jnp.where(col % c <= row, 1.0, 0.0).astype(BF16)
    srow = lax.broadcasted_iota(jnp.int32, (c, kw), 0)
    scol = lax.broadcasted_iota(jnp.int32, (c, kw), 1)
    causal2 = scol % c <= srow
    krow = lax.broadcasted_iota(jnp.int32, (kw, kw), 0)
    kcol = lax.broadcasted_iota(jnp.int32, (kw, kw), 1)
    key_head = (krow < c) == (kcol < GLA_DK)
    vrow = lax.broadcasted_iota(jnp.int32, (kw, vw), 0)
    vcol = lax.broadcasted_iota(jnp.int32, (kw, vw), 1)
    val_head = (vrow < c) == (vcol < GLA_DV)
    trow = lax.broadcasted_iota(jnp.int32, (vw, kw), 0)
    tcol = lax.broadcasted_iota(jnp.int32, (vw, kw), 1)
    same_head = (trow < GLA_DV) == (tcol < GLA_DK)
    gn = gn_ref[...]
    st = st_ref[...]

    for n in range(nchunks):
        sl = pl.ds(n * c, c)
        gcum = _dot(incl3, jnp.concatenate(_split3(la_ref[sl, :]), axis=0))
        glast = gcum[c - 1:c, :]
        q = q_ref[sl, :].astype(F32) * (GLA_DK ** -0.5)
        k = k_ref[sl, :].astype(F32)
        v = v_ref[sl, :]
        q_dec = (q * jnp.exp(gcum)).astype(BF16)
        k_inv = (k * jnp.exp(-gcum)).astype(BF16)
        k_end = (k * jnp.exp(glast - gcum)).astype(BF16)
        k2 = jnp.where(key_head, jnp.concatenate([k_inv, k_inv], axis=0), jnp.zeros((kw, kw), BF16))
        scores = jnp.where(causal2, _dot_nt(q_dec, k2), 0.0)
        v2 = jnp.where(val_head, jnp.concatenate([v, v], axis=0), jnp.zeros((kw, vw), BF16))
        o = _dot(scores.astype(BF16), v2) + _dot_nt(q_dec, st.astype(BF16))
        st = st * jnp.exp(glast) + jnp.where(same_head, _dot_tn(v, k_end), 0.0)
        gate = _silu(go_ref[sl, :].astype(F32))
        outs = []
        for a in range(2):
            hs = slice(a * GLA_DV, (a + 1) * GLA_DV)
            outs.append(_rms(o[:, hs], gn) * gate[:, hs])
        o_ref[sl, :] = jnp.concatenate(outs, axis=1).astype(BF16)
    st_ref[...] = st


def _gla(proj, log_a, gnorm, *, batch, seq, ts):
    t = proj.shape[0]
    nt = seq // ts
    pairs = GLA_HEADS // 2
    kw, vw = 2 * GLA_DK, 2 * GLA_DV
    q0 = 3 * SB_HEADS * SB_DIM // kw
    k0 = q0 + GLA_HEADS * GLA_DK // kw
    v0 = (3 * SB_HEADS * SB_DIM + 2 * GLA_HEADS * GLA_DK) // vw
    g0 = v0 + GLA_HEADS * GLA_DV // vw
    kernel = functools.partial(_gla_kernel, nchunks=ts // GLA_CHUNK)
    return pl.pallas_call(
        kernel,
        out_shape=jax.ShapeDtypeStruct((t, GLA_HEADS * GLA_DV), BF16),
        grid=(batch, pairs, nt),
        in_specs=[
            pl.BlockSpec((ts, kw), lambda b, p, i: (b * nt + i, q0 + p)),
            pl.BlockSpec((ts, kw), lambda b, p, i: (b * nt + i, k0 + p)),
            pl.BlockSpec((ts, vw), lambda b, p, i: (b * nt + i, v0 + p)),
            pl.BlockSpec((ts, vw), lambda b, p, i: (b * nt + i, g0 + p)),
            pl.BlockSpec((ts, kw), lambda b, p, i: (b * nt + i, p)),
            pl.BlockSpec((1, GLA_DV), lambda b, p, i: (0, 0)),
        ],
        out_specs=pl.BlockSpec((ts, vw), lambda b, p, i: (b * nt + i, p)),
        scratch_shapes=[pltpu.VMEM((vw, kw), F32)],
        compiler_params=_params("parallel", "parallel", "arbitrary"),
        name="gla",
    )(proj, proj, proj, proj, log_a, gnorm)


def _ab_outproj_kernel(h_ref, a_ref, b_ref, w_ref, o_ref):
    ka = a_ref.shape[1]
    acc = _dot(a_ref[...], w_ref[0:ka, :]) + _dot(b_ref[...], w_ref[ka:, :])
    o_ref[...] = h_ref[...] + acc


def _ab_outproj(h, a, b, w, *, tm):
    t, d = h.shape
    return pl.pallas_call(
        _ab_outproj_kernel,
        out_shape=jax.ShapeDtypeStruct((t, d), F32),
        grid=(t // tm,),
        in_specs=[
            pl.BlockSpec((tm, d), lambda i: (i, 0)),
            pl.BlockSpec((tm, a.shape[1]), lambda i: (i, 0)),
            pl.BlockSpec((tm, b.shape[1]), lambda i: (i, 0)),
            pl.BlockSpec(w.shape, lambda i: (0, 0)),
        ],
        out_specs=pl.BlockSpec((tm, d), lambda i: (i, 0)),
        compiler_params=_params("parallel"),
        name="ab_outproj",
    )(h, a, b, w)


def _mlp_kernel(h_ref, g_ref, wu_ref, wd_ref, o_ref, xn_ref):
    @pl.when(pl.program_id(1) == 0)
    def _():
        h = h_ref[...]
        xn_ref[...] = _rms(h, g_ref[...]).astype(BF16)
        o_ref[...] = h

    u = jnp.maximum(_dot(xn_ref[...], wu_ref[...]), 0.0)
    o_ref[...] += _dot((u * u).astype(BF16), wd_ref[...])


def _mlp(h, g, wu, wd, layer, *, tm, tf):
    t, d = h.shape
    f = wu.shape[2]
    return pl.pallas_call(
        _mlp_kernel,
        out_shape=jax.ShapeDtypeStruct((t, d), F32),
        grid=(t // tm, f // tf),
        in_specs=[
            pl.BlockSpec((tm, d), lambda i, j: (i, 0)),
            pl.BlockSpec((None, 1, d), lambda i, j: (layer, 0, 0)),
            pl.BlockSpec((None, d, tf), lambda i, j: (layer, 0, j)),
            pl.BlockSpec((None, tf, d), lambda i, j: (layer, j, 0)),
        ],
        out_specs=pl.BlockSpec((tm, d), lambda i, j: (i, 0)),
        scratch_shapes=[pltpu.VMEM((tm, d), BF16)],
        compiler_params=_params("parallel", "arbitrary"),
        name="mlp",
    )(h, g, wu, wd)


def _ple_kernel(h_ref, p_ref, wg_ref, wp_ref, fn_ref, o_ref, *, final_norm):
    h = h_ref[...]
    gate = _dot(h.astype(BF16), wg_ref[...])
    emb = _dot(p_ref[...].astype(BF16), wp_ref[...])
    out = h + emb * (1.0 / (1.0 + jnp.exp(-gate)))
    if final_norm:
        out = _rms(out, fn_ref[...])
    o_ref[...] = out


def _ple(h, p, wg, wp, fn, layer, *, tm, final_norm):
    t, d = h.shape
    e = p.shape[2]
    kernel = functools.partial(_ple_kernel, final_norm=final_norm)
    return pl.pallas_call(
        kernel,
        out_shape=jax.ShapeDtypeStruct((t, d), F32),
        grid=(t // tm,),
        in_specs=[
            pl.BlockSpec((tm, d), lambda i: (i, 0)),
            pl.BlockSpec((None, tm, e), lambda i: (layer, i, 0)),
            pl.BlockSpec((None, d, d), lambda i: (layer, 0, 0), pipeline_mode=pl.Buffered(1)),
            pl.BlockSpec((None, e, d), lambda i: (layer, 0, 0), pipeline_mode=pl.Buffered(1)),
            pl.BlockSpec((1, d), lambda i: (0, 0)),
        ],
        out_specs=pl.BlockSpec((tm, d), lambda i: (i, 0)),
        compiler_params=_params("parallel"),
        name="ple",
    )(h, p, wg, wp, fn)


def _ssd_kernel(xbc_ref, par_ref, dt_ref, sel_ref, y_ref, st_ref, pad_ref, *, nchunks, heads):
    c = SSD_CHUNK
    ts = xbc_ref.shape[0]
    gw = heads * SSD_HEAD_DIM
    n = SSD_STATE
    first = pl.program_id(2) == 0

    @pl.when(first)
    def _():
        st_ref[...] = jnp.zeros_like(st_ref)
        pad_ref[0:CONV_HALO, :] = jnp.zeros((CONV_HALO, pad_ref.shape[1]), F32)

    pad_ref[CONV_HALO:CONV_HALO + ts, :] = xbc_ref[...].astype(F32)
    conv = par_ref[0, 0:1, :] + par_ref[0, SSD_CONV:SSD_CONV + 1, :] * pad_ref[CONV_HALO:CONV_HALO + ts, :]
    for back in range(1, SSD_CONV):
        tap = SSD_CONV - back
        conv = conv + par_ref[0, tap:tap + 1, :] * pad_ref[CONV_HALO - back:CONV_HALO - back + ts, :]
    pad_ref[0:CONV_HALO, :] = pad_ref[ts:ts + CONV_HALO, :]
    xbc = _silu(conv)
    xs_all = xbc[:, :gw]
    bm_all = xbc[:, gw:gw + n].astype(BF16)
    cm_all = xbc[:, gw + n:].astype(BF16)

    row = lax.broadcasted_iota(jnp.int32, (c, c), 0)
    col = lax.broadcasted_iota(jnp.int32, (c, c), 1)
    tril = col <= row
    incl = jnp.where(tril, 1.0, 0.0).astype(BF16)
    lane = lax.broadcasted_iota(jnp.int32, (1, LANES), 1)
    lo_half = lane < SSD_HEAD_DIM
    sel = sel_ref[0]
    dsk = par_ref[0, 5:6, 0:gw]
    dt_bias = par_ref[0, 6:7, 0:LANES]
    a2 = -jnp.exp(par_ref[0, 7:8, 0:LANES]) * LOG2E
    st = st_ref[...]

    for ci in range(nchunks):
        sl = slice(ci * c, (ci + 1) * c)
        xs = xs_all[sl, :]
        bm = bm_all[sl, :]
        cm = cm_all[sl, :]
        z = _dot_split_lhs(dt_ref[sl, :], sel, 2) + dt_bias
        dt = jnp.maximum(z, 0.0) + _softplus_neg_abs(z)
        a_cs = _dot_exact_rhs(incl, dt * a2)
        a_cs_t = a_cs.T
        a_last = a_cs[c - 1:c, :]
        cb = _dot_nt(cm, bm)
        y_pairs, ea_pairs, xds_pairs = [], [], []
        for pr in range(heads // 2):
            cols, ms = [], []
            for j in (2 * pr, 2 * pr + 1):
                a_col = jnp.broadcast_to(a_cs[:, j:j + 1], (c, LANES))
                decay = jnp.where(tril, jnp.exp2(a_col - a_cs_t[j:j + 1, :]), 0.0)
                ms.append((cb * decay).astype(BF16))
                cols.append((a_col, jnp.broadcast_to(dt[:, j:j + 1], (c, LANES)), a_last[:, j:j + 1]))
            pick = lambda f: jnp.where(lo_half, f(cols[0]), f(cols[1]))
            xp = xs[:, pr * LANES:(pr + 1) * LANES] * pick(lambda v: v[1])
            ea_pairs.append(pick(lambda v: jnp.exp2(v[0])))
            xds_pairs.append(xp * pick(lambda v: jnp.exp2(v[2] - v[0])))
            lhs = jnp.concatenate(ms, axis=1)
            rhs = jnp.concatenate([jnp.where(lo_half, xp, 0.0), jnp.where(lo_half, 0.0, xp)],
                                  axis=0).astype(BF16)
            y_pairs.append(_dot(lhs, rhs))
        y_diag = jnp.concatenate(y_pairs, axis=1)
        ea_x = jnp.concatenate(ea_pairs, axis=1)
        y_off = _dot(cm, st.astype(BF16)) * ea_x
        st = st * ea_x[c - 1:c, :] + _dot_tn(bm, jnp.concatenate(xds_pairs, axis=1).astype(BF16))
        y_ref[sl, :] = (y_diag + y_off + dsk * xs).astype(BF16)
    st_ref[...] = st


def _ssd(proj, dt_raw, params, *, batch, seq, ts, inner):
    t = proj.shape[0]
    nt = seq // ts
    g = SSD_GROUPS
    gw = inner // g
    heads = gw // SSD_HEAD_DIM
    n = SSD_STATE
    cw = gw + 2 * n

    sel = np.zeros((g, LANES, LANES), np.float32)
    for j in range(heads):
        for gi in range(g):
            sel[gi, gi * heads + j, j] = 1.0
    sel = jnp.asarray(sel, BF16)

    kernel = functools.partial(_ssd_kernel, nchunks=ts // SSD_CHUNK, heads=heads)
    row = lambda b, gi, i: b * nt + i
    return pl.pallas_call(
        kernel,
        out_shape=jax.ShapeDtypeStruct((t, inner), BF16),
        grid=(batch, g, nt),
        in_specs=[
            pl.BlockSpec((ts, cw), lambda b, gi, i: (row(b, gi, i), gi)),
            pl.BlockSpec((1, 8, cw), lambda b, gi, i: (gi, 0, 0)),
            pl.BlockSpec((ts, LANES), lambda b, gi, i: (row(b, gi, i), 0)),
            pl.BlockSpec((1, LANES, LANES), lambda b, gi, i: (gi, 0, 0)),
        ],
        out_specs=pl.BlockSpec((ts, gw), lambda b, gi, i: (row(b, gi, i), gi)),
        scratch_shapes=[
            pltpu.VMEM((n, gw), F32),
            pltpu.VMEM((ts + CONV_HALO, cw), F32),
        ],
        compiler_params=_params("parallel", "parallel", "arbitrary"),
        name="ssd",
    )(proj, params, dt_raw, sel)


def _ssd_outproj_kernel(h_ref, y_ref, zlo_ref, zhi_ref, g_ref, w_ref, o_ref, *, kc):
    inner = y_ref.shape[1]
    half = zlo_ref.shape[1]
    ssq = None
    acc = None
    for k0 in range(0, inner, kc):
        z_ref, zoff = (zlo_ref, k0) if k0 < half else (zhi_ref, k0 - half)
        z = z_ref[:, zoff:zoff + kc].astype(F32)
        yg = y_ref[:, k0:k0 + kc].astype(F32) * _silu(z)
        part = jnp.sum(yg * yg, axis=-1, keepdims=True)
        ssq = part if ssq is None else ssq + part
        term = _dot((yg * g_ref[:, k0:k0 + kc]).astype(BF16), w_ref[k0:k0 + kc, :])
        acc = term if acc is None else acc + term
    o_ref[...] = h_ref[...] + acc * lax.rsqrt(ssq * (1.0 / inner) + EPS)


def _ssd_outproj(h, y, proj, g, w, *, tm, kc):
    t, d = h.shape
    inner = y.shape[1]
    half = inner // 2
    z0 = (proj.shape[1] - inner) // half
    kernel = functools.partial(_ssd_outproj_kernel, kc=kc)
    return pl.pallas_call(
        kernel,
        out_shape=jax.ShapeDtypeStruct((t, d), F32),
        grid=(t // tm,),
        in_specs=[
            pl.BlockSpec((tm, d), lambda i: (i, 0)),
            pl.BlockSpec((tm, inner), lambda i: (i, 0)),
            pl.BlockSpec((tm, half), lambda i: (i, z0)),
            pl.BlockSpec((tm, half), lambda i: (i, z0 + 1)),
            pl.BlockSpec((1, inner), lambda i: (0, 0)),
            pl.BlockSpec((inner, d), lambda i: (0, 0), pipeline_mode=pl.Buffered(1)),
        ],
        out_specs=pl.BlockSpec((tm, d), lambda i: (i, 0)),
        compiler_params=_params("parallel"),
        name="ssd_outproj",
    )(h, y, proj, proj, g, w)


def _pad_cols(w, n):
    return jnp.pad(w, ((0, 0), (0, n - w.shape[1])))


def _group_cols(a, inner):
    g, n = SSD_GROUPS, SSD_STATE
    gw = inner // g
    b0, c0 = inner, inner + g * n
    out = []
    for gi in range(g):
        out += [a[:, gi * gw:(gi + 1) * gw], a[:, b0 + gi * n:b0 + (gi + 1) * n],
                a[:, c0 + gi * n:c0 + (gi + 1) * n]]
    return out


def _ssd_params(conv_w, conv_b, d_skip, dt_bias, a_log, inner):
    g = SSD_GROUPS
    cw = inner // g + 2 * SSD_STATE
    grouped = jnp.concatenate(_group_cols(jnp.concatenate([conv_b[None, :], conv_w], axis=0), inner),
                              axis=1)
    grouped = grouped.reshape(1 + SSD_CONV, g, cw).transpose(1, 0, 2)
    per_head = lambda v, rep: _pad_cols(jnp.repeat(v, rep).reshape(g, -1), cw)[:, None, :]
    return jnp.concatenate([grouped, per_head(d_skip, SSD_HEAD_DIM), per_head(dt_bias, 1),
                            per_head(a_log, 1)], axis=1).astype(F32)


def kernel(x, p, norm_mix, norm_mlp, ab_w_in, ab_w_gate_up, ab_b_gate, ab_gla_norm, ab_w_out,
           ssd_w_in, ssd_conv_w, ssd_conv_b, ssd_dt_bias, ssd_a_log, ssd_d, ssd_norm, ssd_w_out,
           mlp_w_up, mlp_w_down, ple_w_proj, ple_w_gate, final_norm):
    batch, seq, d = x.shape
    t = batch * seq
    depth = p.shape[0]
    tiles = _tiles(t, seq)
    h = x.reshape(t, d)
    row = lambda v: v.reshape(1, -1).astype(F32)
    mlp_norm = norm_mlp.reshape(depth, 1, d).astype(F32)
    mlp_up, mlp_down = mlp_w_up.astype(BF16), mlp_w_down.astype(BF16)
    ple_gate, ple_proj = ple_w_gate.astype(BF16), ple_w_proj.astype(BF16)
    p_rows = p.reshape(depth, t, -1)

    for layer in range(depth):
        j = layer // 2
        if layer % 2 == 0:
            w = ab_w_in[j].astype(BF16)
            glr0 = 3 * SB_HEADS * SB_DIM + 2 * GLA_HEADS * GLA_DK + GLA_HEADS * GLA_DV
            w_main = jnp.concatenate([w[:, :glr0], w[:, glr0 + GLA_RANK:]], axis=1)
            w_glr = _pad_cols(w[:, glr0:glr0 + GLA_RANK], LANES)
            w_gu = jnp.pad(ab_w_gate_up[j], ((0, LANES - GLA_RANK), (0, 0))).astype(BF16)
            proj, log_a = _ab_inproj(h, row(norm_mix[layer]), w_main, w_glr, w_gu,
                                     row(ab_b_gate[j]), tm=tiles.proj_rows, tn=tiles.proj_cols)
            o_sb = _sb_attention(proj, batch=batch, seq=seq, bq=tiles.sb_queries, bk=tiles.sb_keys)
            o_gla = _gla(proj, log_a, row(ab_gla_norm[j]), batch=batch, seq=seq, ts=tiles.gla_rows)
            h = _ab_outproj(h, o_sb, o_gla, ab_w_out[j].astype(BF16), tm=tiles.out_rows)
        else:
            w = ssd_w_in[j].astype(BF16)
            inner = ssd_w_out.shape[1]
            n_zx = w.shape[1] - inner // SSD_HEAD_DIM
            w_perm = jnp.concatenate(_group_cols(w[:, inner:n_zx], inner) + [w[:, :inner]], axis=1)
            proj, dt_raw = _ssd_inproj(h, row(norm_mix[layer]), w_perm, _pad_cols(w[:, n_zx:], LANES),
                                       tm=tiles.proj_rows, tn=tiles.proj_cols)
            params = _ssd_params(ssd_conv_w[j], ssd_conv_b[j], ssd_d[j], ssd_dt_bias[j],
                                 ssd_a_log[j], inner)
            y = _ssd(proj, dt_raw, params, batch=batch, seq=seq, ts=tiles.ssd_rows, inner=inner)
            h = _ssd_outproj(h, y, proj, row(ssd_norm[j]), ssd_w_out[j].astype(BF16),
                             tm=tiles.ssd_out_rows, kc=tiles.ssd_out_k)
        h = _mlp(h, mlp_norm, mlp_up, mlp_down, layer, tm=tiles.mlp_rows, tf=tiles.mlp_hidden)
        h = _ple(h, p_rows, ple_gate, ple_proj, row(final_norm), layer, tm=tiles.ple_rows,
                 final_norm=layer == depth - 1)
    return h.reshape(batch, seq, d)
```
